```python
import math
import jax
import jax.numpy as jnp
from jax import lax
import numpy as np

D_MODEL = 1024
BATCH = 8
SEQ = 2048
DEPTH = 4
DEC_BATCH = 128
DEC_SEQ = 1
PAST_LEN = 8192
PAGE_SIZE = 128

N_MIXERS = 3
N_A = len(range(0, DEPTH, N_MIXERS))
N_B = len(range(1, DEPTH, N_MIXERS))
N_C = len(range(2, DEPTH, N_MIXERS))
HEAD_DIM = 64
A_HEADS = 16
A_KV_HEADS = 4
A_GROUP = A_HEADS // A_KV_HEADS
IDX_HEADS = 8
IDX_DIM = 64
TOPK_MAX = 256
A_SIZES = (A_HEADS * HEAD_DIM, A_KV_HEADS * HEAD_DIM, A_KV_HEADS * HEAD_DIM,
           IDX_HEADS * IDX_DIM, IDX_DIM, IDX_HEADS)
A_IN = sum(A_SIZES)
B_HEADS = 8
B_KV_HEADS = 4
B_GROUP = B_HEADS // B_KV_HEADS
B_SIZES = (B_HEADS * 2 * HEAD_DIM, B_KV_HEADS * 2 * HEAD_DIM, B_KV_HEADS * 2 * HEAD_DIM)
B_IN = sum(B_SIZES)
C_HEADS = 16
Q_LORA = 384
KV_LORA = 256
NOPE_DIM = 64
ROPE_DIM = 32
V_DIM = 64
C_SIZES = (Q_LORA, KV_LORA, ROPE_DIM)
C_DOWN = sum(C_SIZES)
C_SCALE = (NOPE_DIM + ROPE_DIM) ** -0.5
ROPE_THETA = 10000.0
REL_BUCKETS = 32
REL_MAX_DIST = 128
N_BIAS = 16
D_FF = 2816
CONV_W = 3
EPS = 1e-6
Q_BLOCK = 128
F32 = jnp.float32

kernel_name = 'hybrid_dsa_diff_mla_convffn_step'


def _rmsnorm(x, g):
    xf = x.astype(F32)
    y = xf * lax.rsqrt(jnp.mean(xf * xf, axis=-1, keepdims=True) + EPS)
    return (y * g.astype(F32)).astype(x.dtype)


def _split(z, sizes):
    return jnp.split(z, np.cumsum(sizes)[:-1].tolist(), axis=-1)


def _rel_bucket(dist):
    n = jnp.maximum(dist, 0)
    max_exact = REL_BUCKETS // 2
    nf = jnp.maximum(n, 1).astype(F32)
    large = max_exact + (jnp.log(nf / max_exact) / math.log(REL_MAX_DIST / max_exact)
                         * (REL_BUCKETS - max_exact)).astype(jnp.int32)
    return jnp.where(n < max_exact, n, jnp.minimum(large, REL_BUCKETS - 1))


def _rel_bias(rel_table, dist):
    return rel_table.astype(F32)[_rel_bucket(dist)]


def _rope(x, pos):
    half = ROPE_DIM // 2
    inv = ROPE_THETA ** (-jnp.arange(half, dtype=F32) / half)
    ang = pos.astype(F32)[:, None] * inv
    ang = ang.reshape(ang.shape[:1] + (1,) * (x.ndim - 3) + (half,))
    c, s = jnp.cos(ang), jnp.sin(ang)
    xf = x.astype(F32)
    x1, x2 = xf[..., :half], xf[..., half:]
    return jnp.concatenate([x1 * c - x2 * s, x1 * s + x2 * c], axis=-1).astype(x.dtype)


def _online_init(m_shape, dv):
    return (jnp.full(m_shape, -jnp.inf, F32), jnp.zeros(m_shape, F32),
            jnp.zeros(m_shape + (dv,), F32))


def _online_update(st, logits, pv):
    m, l, acc = st
    m_new = jnp.maximum(m, jnp.max(logits, axis=-1))
    alpha = jnp.exp(m - m_new)
    p = jnp.exp(logits - m_new[..., None])
    return (m_new, l * alpha + jnp.sum(p, axis=-1), acc * alpha[..., None] + pv(p))


def _online_finish(st):
    return st[2] / st[1][..., None]


def _paged_sweep(st, page_table, block_fn):
    def body(carry, j):
        spos = j * PAGE_SIZE + jnp.arange(PAGE_SIZE, dtype=jnp.int32)
        return block_fn(carry, page_table[:, j], spos), None
    st, _ = lax.scan(body, st, jnp.arange(page_table.shape[1], dtype=jnp.int32))
    return st


def _blocks_map(fn, seq):
    starts = jnp.arange(seq // Q_BLOCK, dtype=jnp.int32) * Q_BLOCK
    out = jnp.moveaxis(lax.map(fn, starts), 0, 1)
    return out.reshape((out.shape[0], seq) + out.shape[3:])


def _a_project(h, w_in):
    b, s = h.shape[:2]
    q, k, v, qi, ki, wi = _split(h @ w_in, A_SIZES)
    q = q.reshape(b, s, A_KV_HEADS, A_GROUP, HEAD_DIM) * (HEAD_DIM ** -0.5)
    k = k.reshape(b, s, A_KV_HEADS, HEAD_DIM)
    v = v.reshape(b, s, A_KV_HEADS, HEAD_DIM)
    qi = qi.reshape(b, s, IDX_HEADS, IDX_DIM)
    return q, k, v, qi, ki, wi


def _index_scores(qi, wi, ki):
    dots = jnp.einsum('bqhd,bsd->bqhs', qi, ki, preferred_element_type=F32) * (IDX_DIM ** -0.5)
    return jnp.einsum('bqh,bqhs->bqs', wi.astype(F32) * (IDX_HEADS ** -0.5), jax.nn.relu(dots))


def _sparse_attend(q, k_sel, v_sel, valid, bias):
    b, nq, n = valid.shape
    logits = jnp.einsum('bqkgd,bqnkd->bkgqn', q, k_sel, preferred_element_type=F32)
    bias = bias.reshape(b, nq, n, A_KV_HEADS, A_GROUP).transpose(0, 3, 4, 1, 2)
    logits = jnp.where(valid[:, None, None], logits + bias, -jnp.inf)
    p = jax.nn.softmax(logits, axis=-1)
    o = jnp.einsum('bkgqn,bqnkd->bqkgd', p, v_sel.astype(F32))
    return o.reshape(b, nq, A_HEADS * HEAD_DIM).astype(q.dtype)


def _mixer_a_prompt(h, w_in, w_o, rel_table):
    b, seq = h.shape[:2]
    q, k, v, qi, ki, wi = _a_project(h, w_in)
    topk = min(TOPK_MAX, seq // 4)
    pos = jnp.arange(seq, dtype=jnp.int32)
    bidx = jnp.arange(b)[:, None, None]

    def block(start):
        qb = lax.dynamic_slice_in_dim(q, start, Q_BLOCK, 1)
        qib = lax.dynamic_slice_in_dim(qi, start, Q_BLOCK, 1)
        wib = lax.dynamic_slice_in_dim(wi, start, Q_BLOCK, 1)
        tq = start + jnp.arange(Q_BLOCK, dtype=jnp.int32)
        scores = jnp.where(pos[None, :] <= tq[:, None], _index_scores(qib, wib, ki), -jnp.inf)
        _, sel = lax.top_k(scores, topk)
        valid = sel <= tq[None, :, None]
        bias = _rel_bias(rel_table, tq[None, :, None] - sel)
        return _sparse_attend(qb, k[bidx, sel], v[bidx, sel], valid, bias)

    return _blocks_map(block, seq) @ w_o, (k, v, ki)


def _mixer_a_sample(h, w_in, w_o, rel_table, pool_k, pool_v, pool_i, slot, page_table):
    db, nq = h.shape[:2]
    past = page_table.shape[1] * PAGE_SIZE
    q, k, v, qi, ki, wi = _a_project(h, w_in)
    ki_past = pool_i[slot, page_table].reshape(db, past, IDX_DIM)
    tq = past + jnp.arange(nq, dtype=jnp.int32)
    ar = jnp.arange(nq)
    s_new = jnp.where(ar[None, :] <= ar[:, None], _index_scores(qi, wi, ki), -jnp.inf)
    scores = jnp.concatenate([_index_scores(qi, wi, ki_past), s_new], axis=-1)
    topk = min(TOPK_MAX, (past + nq) // 4)
    _, sel = lax.top_k(scores, topk)
    valid = sel <= tq[None, :, None]
    is_past = (sel < past)[..., None, None]
    sp = jnp.minimum(sel, past - 1)
    bidx = jnp.arange(db)[:, None, None]
    phys = page_table[bidx, sp // PAGE_SIZE]
    off = sp % PAGE_SIZE
    sn = jnp.clip(sel - past, 0, nq - 1)
    k_sel = jnp.where(is_past, pool_k[slot, phys, off], k[bidx, sn])
    v_sel = jnp.where(is_past, pool_v[slot, phys, off], v[bidx, sn])
    bias = _rel_bias(rel_table, tq[None, :, None] - sel)
    return _sparse_attend(q, k_sel, v_sel, valid, bias) @ w_o, (k, v, ki)


def _b_project(h, w_in):
    b, s = h.shape[:2]
    q, k, v = _split(h @ w_in, B_SIZES)
    q = q.reshape(b, s, B_KV_HEADS, B_GROUP, 2, HEAD_DIM) * (HEAD_DIM ** -0.5)
    k = k.reshape(b, s, B_KV_HEADS, 2, HEAD_DIM)
    v = v.reshape(b, s, B_KV_HEADS, 2 * HEAD_DIM)
    return q, k, v


def _b_lambda(lam_p, lam_init):
    lp = lam_p.astype(F32)
    return jnp.exp(jnp.sum(lp[0] * lp[1])) - jnp.exp(jnp.sum(lp[2] * lp[3])) + lam_init


def _b_logits(q, k):
    return jnp.einsum('bqkgjd,bskjd->bkgjqs', q, k, preferred_element_type=F32)


def _b_pv(p, v):
    return jnp.einsum('bkgjqs,bske->bkgjqe', p, v.astype(F32))


def _b_bias(rel_table, dist):
    bias = _rel_bias(rel_table, dist).reshape(dist.shape + (B_KV_HEADS, B_GROUP, 2))
    return bias.transpose(2, 3, 4, 0, 1)


def _b_out(o, lam, lam_init, subln, w_o, dtype):
    b, nq = o.shape[0], o.shape[4]
    d = o[:, :, :, 0] - lam * o[:, :, :, 1]
    d = d.transpose(0, 3, 1, 2, 4).reshape(b, nq, B_HEADS, 2 * HEAD_DIM)
    d = _rmsnorm(d, subln) * (1.0 - lam_init)
    return d.reshape(b, nq, B_HEADS * 2 * HEAD_DIM).astype(dtype) @ w_o


def _mixer_b_prompt(h, w_in, lam_p, subln, w_o, rel_table, lam_init):
    q, k, v = _b_project(h, w_in)
    lam = _b_lambda(lam_p, lam_init)
    seq = h.shape[1]
    pos = jnp.arange(seq, dtype=jnp.int32)

    def block(start):
        qb = lax.dynamic_slice_in_dim(q, start, Q_BLOCK, 1)
        tq = start + jnp.arange(Q_BLOCK, dtype=jnp.int32)
        dist = tq[:, None] - pos[None, :]
        logits = jnp.where(dist >= 0, _b_logits(qb, k) + _b_bias(rel_table, dist), -jnp.inf)
        return _b_out(_b_pv(jax.nn.softmax(logits, axis=-1), v), lam, lam_init, subln, w_o, h.dtype)

    return _blocks_map(block, seq), (k, v)


def _mixer_b_sample(h, w_in, lam_p, subln, w_o, rel_table, lam_init, pool_k, pool_v, slot, page_table):
    q, k, v = _b_project(h, w_in)
    lam = _b_lambda(lam_p, lam_init)
    db, nq = h.shape[:2]
    past = page_table.shape[1] * PAGE_SIZE
    tq = past + jnp.arange(nq, dtype=jnp.int32)
    st = _online_init((db, B_KV_HEADS, B_GROUP, 2, nq), 2 * HEAD_DIM)
    dist = tq[:, None] - tq[None, :]
    logits = jnp.where(dist >= 0, _b_logits(q, k) + _b_bias(rel_table, dist), -jnp.inf)
    st = _online_update(st, logits, lambda p: _b_pv(p, v))

    def page(carry, phys, spos):
        kp = pool_k[slot, phys]
        vp = pool_v[slot, phys]
        lg = _b_logits(q, kp) + _b_bias(rel_table, tq[:, None] - spos[None, :])
        return _online_update(carry, lg, lambda p: _b_pv(p, vp))

    st = _paged_sweep(st, page_table, page)
    return _b_out(_online_finish(st), lam, lam_init, subln, w_o, h.dtype), (k, v)


def _c_project(h, pos, w_down, q_norm, w_uq, kv_norm, w_ukv):
    b, s = h.shape[:2]
    cq, ckv, kpe = _split(h @ w_down, C_SIZES)
    qf = (_rmsnorm(cq, q_norm) @ w_uq).reshape(b, s, C_HEADS, NOPE_DIM + ROPE_DIM)
    q_nope, q_pe = qf[..., :NOPE_DIM], _rope(qf[..., NOPE_DIM:], pos)
    ckv = _rmsnorm(ckv, kv_norm)
    kpe = _rope(kpe, pos)
    w = w_ukv.reshape(KV_LORA, C_HEADS, NOPE_DIM + V_DIM)
    q_lat = jnp.einsum('bshn,rhn->bshr', q_nope, w[..., :NOPE_DIM])
    return q_lat, q_pe, ckv, kpe, w[..., NOPE_DIM:]


def _c_logits(q_lat, q_pe, ckv, kpe):
    return (jnp.einsum('bqhr,bsr->bhqs', q_lat, ckv, preferred_element_type=F32)
            + jnp.einsum('bqhp,bsp->bhqs', q_pe, kpe, preferred_element_type=F32)) * C_SCALE


def _c_pv(p, ckv):
    return jnp.einsum('bhqs,bsr->bhqr', p, ckv.astype(F32))


def _c_out(o_lat, w_uv, w_o, dtype):
    o = jnp.einsum('bhqr,rhv->bqhv', o_lat, w_uv.astype(F32))
    return o.reshape(o.shape[0], o.shape[1], C_HEADS * V_DIM).astype(dtype) @ w_o


def _mixer_c_prompt(h, pos, w_down, q_norm, w_uq, kv_norm, w_ukv, w_o):
    q_lat, q_pe, ckv, kpe, w_uv = _c_project(h, pos, w_down, q_norm, w_uq, kv_norm, w_ukv)

    def block(start):
        ql = lax.dynamic_slice_in_dim(q_lat, start, Q_BLOCK, 1)
        qp = lax.dynamic_slice_in_dim(q_pe, start, Q_BLOCK, 1)
        tq = start + jnp.arange(Q_BLOCK, dtype=jnp.int32)
        logits = jnp.where(pos[None, :] <= tq[:, None], _c_logits(ql, qp, ckv, kpe), -jnp.inf)
        return _c_out(_c_pv(jax.nn.softmax(logits, axis=-1), ckv), w_uv, w_o, h.dtype)

    return _blocks_map(block, h.shape[1]), (ckv, kpe)


def _mixer_c_sample(h, pos, w_down, q_norm, w_uq, kv_norm, w_ukv, w_o, pool_ckv, pool_kpe, slot, page_table):
    q_lat, q_pe, ckv, kpe, w_uv = _c_project(h, pos, w_down, q_norm, w_uq, kv_norm, w_ukv)
    db, nq = h.shape[:2]
    st = _online_init((db, C_HEADS, nq), KV_LORA)
    logits = jnp.where(pos[None, :] <= pos[:, None], _c_logits(q_lat, q_pe, ckv, kpe), -jnp.inf)
    st = _online_update(st, logits, lambda p: _c_pv(p, ckv))

    def page(carry, phys, spos):
        cp = pool_ckv[slot, phys]
        kp = pool_kpe[slot, phys]
        return _online_update(carry, _c_logits(q_lat, q_pe, cp, kp), lambda p: _c_pv(p, cp))

    st = _paged_sweep(st, page_table, page)
    return _c_out(_online_finish(st), w_uv, w_o, h.dtype), (ckv, kpe)


def _conv_ffn(h, prev, w_up, conv_w, conv_b, w_down):
    u = h @ w_up
    s = u.shape[1]
    ext = jnp.concatenate([prev.astype(u.dtype), u], axis=1)
    c = conv_b + conv_w[CONV_W - 1] * ext[:, CONV_W - 1:]
    for j in range(CONV_W - 1):
        c = c + conv_w[j] * ext[:, j:j + s]
    gate, val = jnp.split(c, 2, axis=-1)
    return (jax.nn.silu(gate) * val) @ w_down, ext[:, s:]


def setup_inputs(seed: int = 0) -> dict:
    key = jax.random.key(seed)
    keys = jax.random.split(key, 40)
    counter = iter(range(40))
    n_pages = PAST_LEN // PAGE_SIZE
    n_pool = (DEC_BATCH * n_pages * 5) // 4

    def nrm(shape, scale=1.0):
        return jax.random.normal(keys[next(counter)], shape, F32) * scale

    def gain(shape):
        return 1.0 + 0.05 * jax.random.normal(keys[next(counter)], shape, F32)

    x_prompt = nrm((BATCH, SEQ, D_MODEL))
    x_sample = nrm((DEC_BATCH, DEC_SEQ, D_MODEL))
    cache_a_k = nrm((N_A, n_pool, PAGE_SIZE, A_KV_HEADS, HEAD_DIM))
    cache_a_v = nrm((N_A, n_pool, PAGE_SIZE, A_KV_HEADS, HEAD_DIM))
    cache_a_idx = nrm((N_A, n_pool, PAGE_SIZE, IDX_DIM))
    cache_b_k = nrm((N_B, n_pool, PAGE_SIZE, B_KV_HEADS, 2, HEAD_DIM))
    cache_b_v = nrm((N_B, n_pool, PAGE_SIZE, B_KV_HEADS, 2 * HEAD_DIM))
    cache_c_ckv = nrm((N_C, n_pool, PAGE_SIZE, KV_LORA))
    cache_c_kpe = nrm((N_C, n_pool, PAGE_SIZE, ROPE_DIM))
    state_conv = nrm((DEPTH, DEC_BATCH, CONV_W - 1, 2 * D_FF))
    perm = jax.random.permutation(keys[next(counter)], n_pool)[: DEC_BATCH * n_pages]
    page_table = perm.reshape(DEC_BATCH, n_pages).astype(jnp.int32)
    return {
        'x_prompt': x_prompt, 'x_sample': x_sample,
        'cache_a_k': cache_a_k, 'cache_a_v': cache_a_v, 'cache_a_idx': cache_a_idx,
        'cache_b_k': cache_b_k, 'cache_b_v': cache_b_v,
        'cache_c_ckv': cache_c_ckv, 'cache_c_kpe': cache_c_kpe,
        'state_conv': state_conv, 'page_table': page_table,
        'rel_table': nrm((REL_BUCKETS, N_BIAS), 0.5),
        'norm_mix': gain((DEPTH, D_MODEL)),
        'norm_ffn': gain((DEPTH, D_MODEL)),
        'norm_final': gain((D_MODEL,)),
        'a_w_in': nrm((N_A, D_MODEL, A_IN), D_MODEL ** -0.5),
        'a_w_o': nrm((N_A, A_HEADS * HEAD_DIM, D_MODEL), (A_HEADS * HEAD_DIM) ** -0.5),
        'b_w_in': nrm((N_B, D_MODEL, B_IN), D_MODEL ** -0.5),
        'b_lambda': nrm((N_B, 4, HEAD_DIM), 0.1),
        'b_subln': gain((N_B, 2 * HEAD_DIM)),
        'b_w_o': nrm((N_B, B_HEADS * 2 * HEAD_DIM, D_MODEL), (B_HEADS * 2 * HEAD_DIM) ** -0.5),
        'c_w_down': nrm((N_C, D_MODEL, C_DOWN), D_MODEL ** -0.5),
        'c_q_norm': gain((N_C, Q_LORA)),
        'c_w_uq': nrm((N_C, Q_LORA, C_HEADS * (NOPE_DIM + ROPE_DIM)), Q_LORA ** -0.5),
        'c_kv_norm': gain((N_C, KV_LORA)),
        'c_w_ukv': nrm((N_C, KV_LORA, C_HEADS * (NOPE_DIM + V_DIM)), KV_LORA ** -0.5),
        'c_w_o': nrm((N_C, C_HEADS * V_DIM, D_MODEL), (C_HEADS * V_DIM) ** -0.5),
        'ffn_w_up': nrm((DEPTH, D_MODEL, 2 * D_FF), D_MODEL ** -0.5),
        'ffn_conv_w': nrm((DEPTH, CONV_W, 2 * D_FF), CONV_W ** -0.5),
        'ffn_conv_b': nrm((DEPTH, 2 * D_FF), 0.02),
        'ffn_w_down': nrm((DEPTH, D_FF, D_MODEL), D_FF ** -0.5),
    }


def reference(x_prompt, x_sample, cache_a_k, cache_a_v, cache_a_idx, cache_b_k, cache_b_v,
              cache_c_ckv, cache_c_kpe, state_conv, page_table, rel_table, norm_mix, norm_ffn,
              norm_final, a_w_in, a_w_o, b_w_in, b_lambda, b_subln, b_w_o, c_w_down, c_q_norm,
              c_w_uq, c_kv_norm, c_w_ukv, c_w_o, ffn_w_up, ffn_conv_w, ffn_conv_b, ffn_w_down):
    seq = x_prompt.shape[1]
    past = page_table.shape[1] * PAGE_SIZE
    pos_p = jnp.arange(seq, dtype=jnp.int32)
    pos_s = past + jnp.arange(x_sample.shape[1], dtype=jnp.int32)
    xp, xs = x_prompt, x_sample
    a_p, a_s, b_p, b_s, c_p, c_s, conv_p, conv_s = [], [], [], [], [], [], [], []
    for i in range(DEPTH):
        kind, slot = i % N_MIXERS, i // N_MIXERS
        hp = _rmsnorm(xp, norm_mix[i])
        hs = _rmsnorm(xs, norm_mix[i])
        if kind == 0:
            yp, rp = _mixer_a_prompt(hp, a_w_in[slot], a_w_o[slot], rel_table)
            ys, rs = _mixer_a_sample(hs, a_w_in[slot], a_w_o[slot], rel_table,
                                     cache_a_k, cache_a_v, cache_a_idx, slot, page_table)
            a_p.append(rp)
            a_s.append(rs)
        elif kind == 1:
            lam_init = 0.8 - 0.6 * math.exp(-0.3 * i)
            yp, rp = _mixer_b_prompt(hp, b_w_in[slot], b_lambda[slot], b_subln[slot], b_w_o[slot],
                                     rel_table, lam_init)
            ys, rs = _mixer_b_sample(hs, b_w_in[slot], b_lambda[slot], b_subln[slot], b_w_o[slot],
                                     rel_table, lam_init, cache_b_k, cache_b_v, slot, page_table)
            b_p.append(rp)
            b_s.append(rs)
        else:
            yp, rp = _mixer_c_prompt(hp, pos_p, c_w_down[slot], c_q_norm[slot], c_w_uq[slot],
                                     c_kv_norm[slot], c_w_ukv[slot], c_w_o[slot])
            ys, rs = _mixer_c_sample(hs, pos_s, c_w_down[slot], c_q_norm[slot], c_w_uq[slot],
                                     c_kv_norm[slot], c_w_ukv[slot], c_w_o[slot],
                                     cache_c_ckv, cache_c_kpe, slot, page_table)
            c_p.append(rp)
            c_s.append(rs)
        xp = xp + yp
        xs = xs + ys
        hp = _rmsnorm(xp, norm_ffn[i])
        hs = _rmsnorm(xs, norm_ffn[i])
        prev_p = jnp.zeros((xp.shape[0], CONV_W - 1, 2 * D_FF), xp.dtype)
        yp, cp = _conv_ffn(hp, prev_p, ffn_w_up[i], ffn_conv_w[i], ffn_conv_b[i], ffn_w_down[i])
        ys, cs = _conv_ffn(hs, state_conv[i], ffn_w_up[i], ffn_conv_w[i], ffn_conv_b[i], ffn_w_down[i])
        xp = xp + yp
        xs = xs + ys
        conv_p.append(cp)
        conv_s.append(cs)
    y_prompt = _rmsnorm(xp, norm_final)
    y_sample = _rmsnorm(xs, norm_final)

    def stack(rows, j):
        return jnp.stack([r[j] for r in rows])

    return (y_prompt, y_sample,
            stack(a_p, 0), stack(a_p, 1), stack(a_p, 2), stack(b_p, 0), stack(b_p, 1),
            stack(c_p, 0), stack(c_p, 1), jnp.stack(conv_p),
            stack(a_s, 0), stack(a_s, 1), stack(a_s, 2), stack(b_s, 0), stack(b_s, 1),
            stack(c_s, 0), stack(c_s, 1), jnp.stack(conv_s))
```

```python
import functools
import math

import numpy as np
import jax
import jax.numpy as jnp
from jax import lax
from jax.experimental import pallas as pl
from jax.experimental.pallas import tpu as pltpu

F32 = jnp.float32
BF16 = jnp.bfloat16
I32 = jnp.int32

N_MIXERS = 3
HEAD_DIM = 64
PAGE_SIZE = 128
A_HEADS, A_KV_HEADS = 16, 4
A_GROUP = A_HEADS // A_KV_HEADS
IDX_HEADS, IDX_DIM = 8, 64
TOPK_MAX = 256
A_SIZES = (A_HEADS * HEAD_DIM, A_KV_HEADS * HEAD_DIM, A_KV_HEADS * HEAD_DIM,
           IDX_HEADS * IDX_DIM, IDX_DIM, IDX_HEADS)
B_HEADS, B_KV_HEADS = 8, 4
B_GROUP = B_HEADS // B_KV_HEADS
B_SIZES = (B_HEADS * 2 * HEAD_DIM, B_KV_HEADS * 2 * HEAD_DIM, B_KV_HEADS * 2 * HEAD_DIM)
C_HEADS = 16
Q_LORA, KV_LORA, NOPE_DIM, ROPE_DIM, V_DIM = 384, 256, 64, 32, 64
C_QK = NOPE_DIM + ROPE_DIM
C_SCALE = C_QK ** -0.5
ROPE_THETA = 10000.0
REL_BUCKETS, REL_MAX_DIST, N_BIAS = 32, 128, 16
CONV_W = 3
EPS = 1e-6

LANE = 128
TQ = 256
KC = 256
FFN_CK = 256
PAGES_PER_STEP = 8
MASKED = -1e30
INT_MIN = -2 ** 31
VMEM_LIMIT = 56 * 1024 * 1024


def _cparams(sem):
    return pltpu.CompilerParams(dimension_semantics=sem, vmem_limit_bytes=VMEM_LIMIT)


def _rms(x, g):
    return x * lax.rsqrt(jnp.mean(x * x, axis=-1, keepdims=True) + EPS) * g


def _rel_bucket_np(dist):
    n = np.maximum(np.asarray(dist, np.int64), 0)
    max_exact = REL_BUCKETS // 2
    nf = np.maximum(n, 1).astype(np.float32)
    large = max_exact + (np.log(nf / np.float32(max_exact)) / np.float32(math.log(REL_MAX_DIST / max_exact))
                         * np.float32(REL_BUCKETS - max_exact)).astype(np.int32)
    return np.where(n < max_exact, n, np.minimum(large, REL_BUCKETS - 1)).astype(np.int32)


def _norm_proj_kernel(x_ref, g_ref, *refs, n_out, scales):
    w_refs, o_refs = refs[:n_out], refs[n_out:]
    hb = _rms(x_ref[...], g_ref[...]).astype(BF16)
    for w_ref, o_ref, s in zip(w_refs, o_refs, scales):
        y = jnp.dot(hb, w_ref[...], preferred_element_type=F32)
        if s != 1.0:
            y = y * s
        o_ref[...] = y.astype(o_ref.dtype)


def _norm_proj(x, g, ws, dtypes, scales, tm):
    m, d = x.shape
    n_out = len(ws)
    return pl.pallas_call(
        functools.partial(_norm_proj_kernel, n_out=n_out, scales=tuple(scales)),
        grid=(m // tm,),
        in_specs=[pl.BlockSpec((tm, d), lambda i: (i, 0)), pl.BlockSpec((1, d), lambda i: (0, 0))]
        + [pl.BlockSpec(w.shape, lambda i: (0, 0)) for w in ws],
        out_specs=[pl.BlockSpec((tm, w.shape[1]), lambda i: (i, 0)) for w in ws],
        out_shape=[jax.ShapeDtypeStruct((m, w.shape[1]), dt) for w, dt in zip(ws, dtypes)],
        compiler_params=_cparams(("parallel",)),
        name="norm_proj",
    )(x, g.reshape(1, d), *ws)


def _out_proj_kernel(a_ref, w_ref, x_ref, o_ref):
    o_ref[...] = x_ref[...] + jnp.dot(a_ref[...].astype(BF16), w_ref[...], preferred_element_type=F32)


def _out_proj(a, w, x, tm):
    m, k = a.shape
    d = w.shape[1]
    return pl.pallas_call(
        _out_proj_kernel,
        grid=(m // tm,),
        in_specs=[pl.BlockSpec((tm, k), lambda i: (i, 0)), pl.BlockSpec((k, d), lambda i: (0, 0)),
                  pl.BlockSpec((tm, d), lambda i: (i, 0))],
        out_specs=pl.BlockSpec((tm, d), lambda i: (i, 0)),
        out_shape=jax.ShapeDtypeStruct((m, d), F32),
        compiler_params=_cparams(("parallel",)),
        name="out_proj",
    )(a, w, x)


def _b_post_kernel(o0_ref, o1_ref, lam_ref, sub_ref, w_ref, x_ref, o_ref, *, lam_init, n_heads):
    lp = lam_ref[...]
    lam = (jnp.exp(jnp.sum(lp[0:1] * lp[1:2], axis=-1, keepdims=True))
           - jnp.exp(jnp.sum(lp[2:3] * lp[3:4], axis=-1, keepdims=True)) + lam_init)
    d = o0_ref[...] - lam * o1_ref[...]
    hd = 2 * HEAD_DIM
    parts = []
    for h in range(n_heads):
        dh = d[:, h * hd:(h + 1) * hd]
        parts.append(_rms(dh, sub_ref[...]) * (1.0 - lam_init))
    dn = jnp.concatenate(parts, axis=-1).astype(BF16)
    o_ref[...] = x_ref[...] + jnp.dot(dn, w_ref[...], preferred_element_type=F32)


def _b_post(o0, o1, lam_p, subln, w_o, x, lam_init, tm):
    m, k = o0.shape
    d = w_o.shape[1]
    return pl.pallas_call(
        functools.partial(_b_post_kernel, lam_init=lam_init, n_heads=B_HEADS),
        grid=(m // tm,),
        in_specs=[pl.BlockSpec((tm, k), lambda i: (i, 0)), pl.BlockSpec((tm, k), lambda i: (i, 0)),
                  pl.BlockSpec(lam_p.shape, lambda i: (0, 0)), pl.BlockSpec((1, 2 * HEAD_DIM), lambda i: (0, 0)),
                  pl.BlockSpec((k, d), lambda i: (0, 0)), pl.BlockSpec((tm, d), lambda i: (i, 0))],
        out_specs=pl.BlockSpec((tm, d), lambda i: (i, 0)),
        out_shape=jax.ShapeDtypeStruct((m, d), F32),
        compiler_params=_cparams(("parallel",)),
        name="b_post",
    )(o0, o1, lam_p, subln.reshape(1, -1), w_o, x)


def _swap_halves(x, half):
    n = x.shape[-1]
    lane = lax.broadcasted_iota(I32, x.shape, x.ndim - 1)
    fwd = pltpu.roll(x, n - half, x.ndim - 1)
    bwd = pltpu.roll(x, half, x.ndim - 1)
    return jnp.where((lane // half) % 2 == 0, fwd, bwd)


def _c_proj_kernel(x_ref, g_ref, wd_ref, qn_ref, wuq_ref, kvn_ref, wuk_ref, wuv_ref,
                   cq_ref, sq_ref, ck_ref, sk_ref,
                   qn_o, qp_o, ckv_o, kpe_o, kn_o, v_o):
    hb = _rms(x_ref[...], g_ref[...]).astype(BF16)
    down = jnp.dot(hb, wd_ref[...], preferred_element_type=F32)
    cq = down[:, :Q_LORA]
    ckv = down[:, Q_LORA:Q_LORA + KV_LORA]
    kp = down[:, Q_LORA + KV_LORA:]
    qf = jnp.dot(_rms(cq, qn_ref[...]).astype(BF16), wuq_ref[...], preferred_element_type=F32)
    nq = C_HEADS * NOPE_DIM
    qn_o[...] = (qf[:, :nq] * C_SCALE).astype(qn_o.dtype)
    qp = qf[:, nq:]
    qp = qp * cq_ref[...] + _swap_halves(qp, ROPE_DIM // 2) * sq_ref[...]
    qp_o[...] = (qp * C_SCALE).astype(qp_o.dtype)
    ckv_n = _rms(ckv, kvn_ref[...])
    ckv_o[...] = ckv_n
    kp = kp * ck_ref[...] + _swap_halves(kp, ROPE_DIM // 2) * sk_ref[...]
    kpe_o[...] = kp[:, :ROPE_DIM]
    cb = ckv_n.astype(BF16)
    kn_o[...] = jnp.dot(cb, wuk_ref[...], preferred_element_type=F32).astype(kn_o.dtype)
    v_o[...] = jnp.dot(cb, wuv_ref[...], preferred_element_type=F32).astype(v_o.dtype)


def _c_proj(x, g, w_down, q_norm, w_uq_perm, kv_norm, w_uk, w_uv, cos_q, sin_q, cos_k, sin_k, tm, rows_per_seq):
    m, d = x.shape
    nblk = rows_per_seq // tm

    def full(a):
        return pl.BlockSpec(a.shape, lambda i: (0, 0))

    def rows(n):
        return pl.BlockSpec((tm, n), lambda i: (i, 0))

    def pos(n):
        return pl.BlockSpec((tm, n), lambda i: (i % nblk, 0))

    hq, hp = C_HEADS * NOPE_DIM, C_HEADS * ROPE_DIM
    return pl.pallas_call(
        _c_proj_kernel,
        grid=(m // tm,),
        in_specs=[rows(d), full(g.reshape(1, d)), full(w_down), full(q_norm.reshape(1, -1)), full(w_uq_perm),
                  full(kv_norm.reshape(1, -1)), full(w_uk), full(w_uv), pos(hp), pos(hp), pos(LANE), pos(LANE)],
        out_specs=[rows(hq), rows(hp), rows(KV_LORA), rows(ROPE_DIM), rows(hq), rows(hq)],
        out_shape=[jax.ShapeDtypeStruct((m, hq), BF16), jax.ShapeDtypeStruct((m, hp), BF16),
                   jax.ShapeDtypeStruct((m, KV_LORA), F32), jax.ShapeDtypeStruct((m, ROPE_DIM), F32),
                   jax.ShapeDtypeStruct((m, hq), BF16), jax.ShapeDtypeStruct((m, hq), BF16)],
        compiler_params=_cparams(("parallel",)),
        name="c_proj",
    )(x, g.reshape(1, d), w_down, q_norm.reshape(1, -1), w_uq_perm, kv_norm.reshape(1, -1), w_uk, w_uv,
      cos_q, sin_q, cos_k, sin_k)


def _rope_tables(positions):
    half = ROPE_DIM // 2
    inv = ROPE_THETA ** (-np.arange(half, dtype=np.float32) / np.float32(half))
    ang = np.asarray(positions, np.float32)[:, None] * inv[None, :]
    c, s = np.cos(ang).astype(np.float32), np.sin(ang).astype(np.float32)
    cos1 = np.concatenate([c, c], axis=-1)
    sin1 = np.concatenate([-s, s], axis=-1)
    return (jnp.asarray(np.tile(cos1, (1, C_HEADS))), jnp.asarray(np.tile(sin1, (1, C_HEADS))),
            jnp.asarray(np.tile(cos1, (1, LANE // ROPE_DIM))), jnp.asarray(np.tile(sin1, (1, LANE // ROPE_DIM))))


def _silu(x):
    return x * (1.0 / (1.0 + jnp.exp(-x)))


def _ffn_prompt_kernel(x_ref, g_ref, wg_ref, wv_ref, cw_ref, wd_ref, gf_ref,
                       y_ref, st_ref, carry_ref, acc_ref, *, nchunk, final_norm):
    t = pl.program_id(1)
    nt = pl.num_programs(1)

    @pl.when(t == 0)
    def _():
        carry_ref[...] = jnp.zeros_like(carry_ref)

    x = x_ref[0]
    tm = x.shape[0]
    hb = _rms(x, g_ref[...]).astype(BF16)
    acc_ref[...] = x
    row = lax.broadcasted_iota(I32, (tm, FFN_CK), 0)

    def conv_part(c, part, w_ref):
        u = jnp.dot(hb, w_ref[c], preferred_element_type=F32)
        idx = part * nchunk + c
        prev = carry_ref[idx]
        p1, p2 = prev[7:8], prev[6:7]
        u1 = jnp.where(row == 0, p1, pltpu.roll(u, 1, 0))
        u2 = jnp.where(row == 0, p2, jnp.where(row == 1, p1, pltpu.roll(u, 2, 0)))
        cw = cw_ref[idx]
        carry_ref[idx, 6:8, :] = u[tm - 2:tm]
        return cw[3:4] + cw[2:3] * u + cw[1:2] * u1 + cw[0:1] * u2

    def body(c, _):
        gate = conv_part(c, 0, wg_ref)
        val = conv_part(c, 1, wv_ref)
        act = (_silu(gate) * val).astype(BF16)
        acc_ref[...] += jnp.dot(act, wd_ref[c], preferred_element_type=F32)
        return 0

    lax.fori_loop(0, nchunk, body, 0)
    y = acc_ref[...]
    if final_norm:
        y = _rms(y, gf_ref[...])
    y_ref[0] = y

    @pl.when(t == nt - 1)
    def _():
        st_ref[0] = carry_ref[...]


def _ffn_weights(w_up, conv_w, conv_b, w_down):
    d, two_f = w_up.shape
    f = two_f // 2
    nc = f // FFN_CK
    wu = w_up.astype(BF16)
    wg = wu[:, :f].reshape(d, nc, FFN_CK).transpose(1, 0, 2)
    wv = wu[:, f:].reshape(d, nc, FFN_CK).transpose(1, 0, 2)
    wd = w_down.astype(BF16).reshape(nc, FFN_CK, d)
    cw = jnp.concatenate([conv_w, conv_b[None], jnp.zeros((8 - CONV_W - 1, two_f), F32)], axis=0)
    cw = cw.reshape(8, 2 * nc, FFN_CK).transpose(1, 0, 2)
    return wg, wv, wd, cw, nc


def _ffn_prompt(x, g, fw, g_final, final_norm, tm):
    b, s, d = x.shape
    wg, wv, wd, cw, nc = fw
    y, st = pl.pallas_call(
        functools.partial(_ffn_prompt_kernel, nchunk=nc, final_norm=final_norm),
        grid=(b, s // tm),
        in_specs=[pl.BlockSpec((1, tm, d), lambda i, t: (i, t, 0)), pl.BlockSpec((1, d), lambda i, t: (0, 0)),
                  pl.BlockSpec(wg.shape, lambda i, t: (0, 0, 0)), pl.BlockSpec(wv.shape, lambda i, t: (0, 0, 0)),
                  pl.BlockSpec(cw.shape, lambda i, t: (0, 0, 0)),
                  pl.BlockSpec(wd.shape, lambda i, t: (0, 0, 0)), pl.BlockSpec((1, d), lambda i, t: (0, 0))],
        out_specs=[pl.BlockSpec((1, tm, d), lambda i, t: (i, t, 0)),
                   pl.BlockSpec((1, 2 * nc, 8, FFN_CK), lambda i, t: (i, 0, 0, 0))],
        out_shape=[jax.ShapeDtypeStruct((b, s, d), F32), jax.ShapeDtypeStruct((b, 2 * nc, 8, FFN_CK), F32)],
        scratch_shapes=[pltpu.VMEM((2 * nc, 8, FFN_CK), F32), pltpu.VMEM((tm, d), F32)],
        compiler_params=_cparams(("parallel", "arbitrary")),
        name="ffn_prompt",
    )(x, g.reshape(1, d), wg, wv, cw, wd, g_final.reshape(1, d))
    st = st[:, :, 6:8, :].transpose(0, 2, 1, 3).reshape(b, 2, 2 * nc * FFN_CK)
    return y, st


def _ffn_sample_kernel(x_ref, g_ref, wg_ref, wv_ref, cw_ref, wd_ref, gf_ref, p0_ref, p1_ref,
                       y_ref, u_ref, acc_ref, *, nchunk, final_norm):
    x = x_ref[...]
    hb = _rms(x, g_ref[...]).astype(BF16)
    acc_ref[...] = x

    def conv_part(c, part, w_ref):
        u = jnp.dot(hb, w_ref[c], preferred_element_type=F32)
        idx = part * nchunk + c
        u_ref[idx] = u
        cw = cw_ref[idx]
        return cw[3:4] + cw[2:3] * u + cw[1:2] * p1_ref[idx] + cw[0:1] * p0_ref[idx]

    def body(c, _):
        gate = conv_part(c, 0, wg_ref)
        val = conv_part(c, 1, wv_ref)
        act = (_silu(gate) * val).astype(BF16)
        acc_ref[...] += jnp.dot(act, wd_ref[c], preferred_element_type=F32)
        return 0

    lax.fori_loop(0, nchunk, body, 0)
    y = acc_ref[...]
    if final_norm:
        y = _rms(y, gf_ref[...])
    y_ref[...] = y


def _ffn_sample(x, g, fw, g_final, final_norm, state):
    m, d = x.shape
    wg, wv, wd, cw, nc = fw
    st = state.reshape(m, 2, 2 * nc, FFN_CK).transpose(1, 2, 0, 3)
    full3 = lambda a: pl.BlockSpec(a.shape, lambda i: (0, 0, 0))
    full2 = lambda a: pl.BlockSpec(a.shape, lambda i: (0, 0))
    p0, p1 = st[0], st[1]
    y, u = pl.pallas_call(
        functools.partial(_ffn_sample_kernel, nchunk=nc, final_norm=final_norm),
        grid=(1,),
        in_specs=[full2(x), pl.BlockSpec((1, d), lambda i: (0, 0)), full3(wg), full3(wv), full3(cw), full3(wd),
                  pl.BlockSpec((1, d), lambda i: (0, 0)), full3(p0), full3(p1)],
        out_specs=[full2(x), full3(p0)],
        out_shape=[jax.ShapeDtypeStruct((m, d), F32), jax.ShapeDtypeStruct(p0.shape, F32)],
        scratch_shapes=[pltpu.VMEM((m, d), F32)],
        compiler_params=_cparams(("arbitrary",)),
        name="ffn_sample",
    )(x, g.reshape(1, d), wg, wv, cw, wd, g_final.reshape(1, d), p0, p1)
    u = u.transpose(1, 0, 2).reshape(m, 2 * nc * FFN_CK)
    return y, jnp.stack([state[:, 1], u], axis=1)


def _sortable(s):
    b = lax.bitcast_convert_type(s + 0.0, I32)
    return jnp.where(b < 0, b ^ 0x7FFFFFFF, b)


_KEY_NEG_INF = int(np.int32(np.array(-np.inf, np.float32).view(np.int32)) ^ np.int32(0x7FFFFFFF))


def _row_count(mask):
    return jnp.sum(jnp.where(mask, 1.0, 0.0), axis=-1, keepdims=True)


def _kth_largest(key_ref, nchunks, rows, k):
    def count_ge(cand):
        return lax.fori_loop(0, nchunks, lambda c, acc: acc + _row_count(key_ref[c] >= cand),
                             jnp.zeros((rows, 1), F32))

    def bit_step(i, carry):
        thr, cnt_thr = carry
        cand = thr + lax.shift_left(jnp.int32(1), 31 - i)
        cnt = count_ge(cand)
        ok = cnt >= k
        return jnp.where(ok, cand, thr), jnp.where(ok, cnt, cnt_thr)

    total = (nchunks * key_ref.shape[2]).astype(F32) if not isinstance(nchunks, int) else float(nchunks * key_ref.shape[2])
    init = (jnp.full((rows, 1), INT_MIN, I32), jnp.zeros((rows, 1), F32) + total)
    return lax.fori_loop(0, 32, bit_step, init)


def _write_select_mask(key_ref, ma_ref, nchunks, rows, k, thr, cnt_thr, tri_ref):
    kc = key_ref.shape[2]
    need_ties = jnp.max(jnp.where((cnt_thr > k) & (thr > _KEY_NEG_INF), 1.0, 0.0)) > 0.0

    @pl.when(jnp.logical_not(need_ties))
    def _():
        def body(c, _):
            ma_ref[c] = jnp.where(key_ref[c] >= thr, 0.0, MASKED)
            return 0
        lax.fori_loop(0, nchunks, body, 0)

    @pl.when(need_ties)
    def _():
        n_gt = lax.fori_loop(0, nchunks, lambda c, acc: acc + _row_count(key_ref[c] > thr),
                             jnp.zeros((rows, 1), F32))
        room = k - n_gt

        def body(c, seen):
            key = key_ref[c]
            eq = key == thr
            eqb = jnp.where(eq, 1.0, 0.0).astype(BF16)
            prefix = seen + jnp.dot(eqb, tri_ref[...], preferred_element_type=F32)
            keep = (key > thr) | (eq & (prefix <= room))
            ma_ref[c] = jnp.where(keep, 0.0, MASKED)
            return seen + _row_count(eq)
        lax.fori_loop(0, nchunks, body, jnp.zeros((rows, 1), F32))


def _tri(n):
    return jnp.asarray(np.triu(np.ones((n, n), np.float32)), BF16)


def _osm(s, shift, state, v):
    m, l, acc = state
    mc = jnp.max(s, axis=-1, keepdims=True) + shift
    m_new = jnp.maximum(m, mc)
    alpha = jnp.exp(m - m_new)
    p = jnp.exp(s - (m_new - shift))
    l_new = alpha * l + jnp.sum(p, axis=-1, keepdims=True)
    acc_new = alpha * acc + jnp.dot(p.astype(BF16), v, preferred_element_type=F32)
    return m_new, l_new, acc_new


def _osm_init(rows, dv):
    return (jnp.full((rows, 1), -jnp.inf, F32), jnp.zeros((rows, 1), F32), jnp.zeros((rows, dv), F32))


def _far_shift(far_ref, first, n, rows_each):
    return jnp.concatenate([jnp.full((rows_each, 1), far_ref[first + i], F32) for i in range(n)], axis=0)


def _causal_attend(q, kt_at, v_at, tile_at, shift, qb, ma_at=None):
    rows = q.shape[0]
    n_stack = rows // TQ
    dv = v_at(0).shape[-1]

    def logits(c):
        s = jnp.dot(q, kt_at(c), preferred_element_type=F32)
        if ma_at is not None:
            s = (s.reshape(n_stack, TQ, KC) + ma_at(c)[None]).reshape(rows, KC)
        return s

    def far(c, st):
        return _osm(logits(c), shift, st, v_at(c))

    st = lax.fori_loop(0, jnp.maximum(qb - 1, 0), far, _osm_init(rows, dv))

    def near(c, w, st):
        s = logits(c)
        if tile_at is not None:
            s = s + tile_at(w)
        return _osm(s, 0.0, st, v_at(c))

    st = lax.cond(qb >= 1, lambda st: near(qb - 1, 0, st), lambda st: st, st)
    m, l, acc = near(qb, 1, st)
    return acc / l


def _a_prompt_kernel(far_ref, qi_ref, wi_ref, kit_ref, q_ref, kt_ref, v_ref, tile_ref, tri_ref,
                     o_ref, key_ref, ma_ref, *, topk):
    qb = pl.program_id(1)
    qi = qi_ref[0, 0]
    wi = wi_ref[0] * (IDX_HEADS ** -0.5)

    def score(c):
        d = jnp.maximum(jnp.dot(qi, kit_ref[0, c], preferred_element_type=F32), 0.0)
        d = d.reshape(IDX_HEADS, TQ, KC)
        s = d[0] * wi[:, 0:1]
        for h in range(1, IDX_HEADS):
            s = s + d[h] * wi[:, h:h + 1]
        return s

    def score_body(c, _):
        key_ref[c] = _sortable(score(c))
        return 0

    lax.fori_loop(0, qb, score_body, 0)
    causal = (lax.broadcasted_iota(I32, (TQ, KC), 1) <= lax.broadcasted_iota(I32, (TQ, KC), 0))
    key_ref[qb] = _sortable(jnp.where(causal, score(qb), -jnp.inf))
    thr, cnt_thr = _kth_largest(key_ref, qb + 1, TQ, topk)
    _write_select_mask(key_ref, ma_ref, qb + 1, TQ, topk, thr, cnt_thr, tri_ref)
    for kv in range(A_KV_HEADS):
        o = _causal_attend(
            q_ref[0, 0, kv],
            lambda c: kt_ref[0, kv, c], lambda c: v_ref[0, kv, c],
            lambda w: tile_ref[kv, w],
            _far_shift(far_ref, kv * A_GROUP, A_GROUP, TQ), qb,
            ma_at=lambda c: ma_ref[c])
        for g in range(A_GROUP):
            h = kv * A_GROUP + g
            o_ref[0, :, h * HEAD_DIM:(h + 1) * HEAD_DIM] = o[g * TQ:(g + 1) * TQ].astype(o_ref.dtype)


def _a_prompt_attn(far, qi, wi, kit, q, kt, v, tiles, topk):
    b, nq = q.shape[0], q.shape[1]
    nc = kt.shape[2]
    s = nq * TQ
    grid_spec = pltpu.PrefetchScalarGridSpec(
        num_scalar_prefetch=1,
        grid=(b, nq),
        in_specs=[
            pl.BlockSpec((1, 1, IDX_HEADS * TQ, IDX_DIM), lambda i, j, f: (i, j, 0, 0)),
            pl.BlockSpec((1, TQ, IDX_HEADS), lambda i, j, f: (i, j, 0)),
            pl.BlockSpec((1, nc, IDX_DIM, KC), lambda i, j, f: (i, 0, 0, 0)),
            pl.BlockSpec((1, 1, A_KV_HEADS, A_GROUP * TQ, HEAD_DIM), lambda i, j, f: (i, j, 0, 0, 0)),
            pl.BlockSpec((1, A_KV_HEADS, nc, HEAD_DIM, KC), lambda i, j, f: (i, 0, 0, 0, 0)),
            pl.BlockSpec((1, A_KV_HEADS, nc, KC, HEAD_DIM), lambda i, j, f: (i, 0, 0, 0, 0)),
            pl.BlockSpec(tiles.shape, lambda i, j, f: (0, 0, 0, 0)),
            pl.BlockSpec((KC, KC), lambda i, j, f: (0, 0)),
        ],
        out_specs=pl.BlockSpec((1, TQ, A_HEADS * HEAD_DIM), lambda i, j, f: (i, j, 0)),
        scratch_shapes=[pltpu.VMEM((nc, TQ, KC), I32), pltpu.VMEM((nc, TQ, KC), F32)],
    )
    return pl.pallas_call(
        functools.partial(_a_prompt_kernel, topk=topk),
        grid_spec=grid_spec,
        out_shape=jax.ShapeDtypeStruct((b, s, A_HEADS * HEAD_DIM), BF16),
        compiler_params=_cparams(("parallel", "arbitrary")),
        name="a_prompt_attn",
    )(far, qi, wi, kit, q, kt, v, tiles, _tri(KC))


def _b_prompt_kernel(far_ref, q_ref, kt_ref, v_ref, tile_ref, o0_ref, o1_ref):
    qb = pl.program_id(1)
    hd = 2 * HEAD_DIM
    for kv in range(B_KV_HEADS):
        for j in range(2):
            grp = kv * 2 + j
            o = _causal_attend(
                q_ref[0, 0, grp],
                lambda c: kt_ref[0, grp, c], lambda c: v_ref[0, kv, c],
                lambda w: tile_ref[grp, w],
                _far_shift(far_ref, grp * B_GROUP, B_GROUP, TQ), qb)
            o_ref = o0_ref if j == 0 else o1_ref
            for g in range(B_GROUP):
                h = kv * B_GROUP + g
                o_ref[0, :, h * hd:(h + 1) * hd] = o[g * TQ:(g + 1) * TQ]


def _b_prompt_attn(far, q, kt, v, tiles):
    b, nq = q.shape[0], q.shape[1]
    nc = kt.shape[2]
    s = nq * TQ
    hd = 2 * HEAD_DIM
    ngrp = B_KV_HEADS * 2
    grid_spec = pltpu.PrefetchScalarGridSpec(
        num_scalar_prefetch=1,
        grid=(b, nq),
        in_specs=[
            pl.BlockSpec((1, 1, ngrp, B_GROUP * TQ, HEAD_DIM), lambda i, j, f: (i, j, 0, 0, 0)),
            pl.BlockSpec((1, ngrp, nc, HEAD_DIM, KC), lambda i, j, f: (i, 0, 0, 0, 0)),
            pl.BlockSpec((1, B_KV_HEADS, nc, KC, hd), lambda i, j, f: (i, 0, 0, 0, 0)),
            pl.BlockSpec(tiles.shape, lambda i, j, f: (0, 0, 0, 0)),
        ],
        out_specs=[pl.BlockSpec((1, TQ, B_HEADS * hd), lambda i, j, f: (i, j, 0))] * 2,
    )
    return pl.pallas_call(
        _b_prompt_kernel,
        grid_spec=grid_spec,
        out_shape=[jax.ShapeDtypeStruct((b, s, B_HEADS * hd), F32)] * 2,
        compiler_params=_cparams(("parallel", "arbitrary")),
        name="b_prompt_attn",
    )(far, q, kt, v, tiles)


def _c_prompt_kernel(far_ref, q_ref, kt_ref, v_ref, tile_ref, o_ref):
    qb = pl.program_id(1)
    zero = jnp.zeros((TQ, 1), F32)
    for h in range(C_HEADS):
        o = _causal_attend(
            q_ref[0, 0, h],
            lambda c: kt_ref[0, h, c], lambda c: v_ref[0, h, c],
            lambda w: tile_ref[w], zero, qb)
        o_ref[0, :, h * V_DIM:(h + 1) * V_DIM] = o.astype(o_ref.dtype)


def _c_prompt_attn(q, kt, v, tiles):
    b, nq = q.shape[0], q.shape[1]
    nc = kt.shape[2]
    s = nq * TQ
    grid_spec = pltpu.PrefetchScalarGridSpec(
        num_scalar_prefetch=1,
        grid=(b, nq),
        in_specs=[
            pl.BlockSpec((1, 1, C_HEADS, TQ, C_QK), lambda i, j, f: (i, j, 0, 0, 0)),
            pl.BlockSpec((1, C_HEADS, nc, C_QK, KC), lambda i, j, f: (i, 0, 0, 0, 0)),
            pl.BlockSpec((1, C_HEADS, nc, KC, V_DIM), lambda i, j, f: (i, 0, 0, 0, 0)),
            pl.BlockSpec(tiles.shape, lambda i, j, f: (0, 0, 0)),
        ],
        out_specs=pl.BlockSpec((1, TQ, C_HEADS * V_DIM), lambda i, j, f: (i, j, 0)),
    )
    return pl.pallas_call(
        _c_prompt_kernel,
        grid_spec=grid_spec,
        out_shape=jax.ShapeDtypeStruct((b, s, C_HEADS * V_DIM), BF16),
        compiler_params=_cparams(("parallel", "arbitrary")),
        name="c_prompt_attn",
    )(jnp.zeros((1,), F32), q, kt, v, tiles)


def _bias_tiles(rel_table, cols):
    i = np.arange(TQ)[:, None]
    j = np.arange(KC)[None, :]
    dist = np.stack([i - j + KC, i - j])
    bucket = _rel_bucket_np(dist)
    assert np.all(_rel_bucket_np(np.arange(KC + 1, 16 * KC)) == REL_BUCKETS - 1)
    tab = rel_table.astype(F32)[:, np.asarray(cols)]
    tiles = jnp.transpose(tab[bucket], (3, 0, 1, 2))
    causal = jnp.asarray((dist >= 0)[None])
    return jnp.where(causal, tiles, MASKED), tab[REL_BUCKETS - 1]


def _page_specs(arr_ndim_tail, block_tail, slot, n_pages, pps):
    specs = []
    for p in range(pps):
        def imap(i, g, pt, p=p):
            return (slot, pt[i * n_pages + g * pps + p]) + (0,) * arr_ndim_tail
        specs.append(pl.BlockSpec((None, None) + block_tail, imap))
    return specs


def _a_idx_kernel(pt_ref, qi_ref, wi_ref, *refs, pps):
    page_refs, o_ref = refs[:pps], refs[pps]
    qi = qi_ref[0]
    wi = wi_ref[0] * (IDX_HEADS ** -0.5)
    outs = []
    for p in range(pps):
        d = lax.dot_general(qi, page_refs[p][...].astype(BF16), (((1,), (1,)), ((), ())),
                            preferred_element_type=F32)
        outs.append(jnp.sum(jnp.maximum(d, 0.0) * wi, axis=0, keepdims=True))
    o_ref[0] = jnp.concatenate(outs, axis=-1)


def _a_idx_scores(page_table, qi, wi, pool_idx, slot):
    db = qi.shape[0]
    n_pages = page_table.shape[1]
    pps = PAGES_PER_STEP
    grid_spec = pltpu.PrefetchScalarGridSpec(
        num_scalar_prefetch=1,
        grid=(db, n_pages // pps),
        in_specs=[pl.BlockSpec((1, 2 * IDX_HEADS, IDX_DIM), lambda i, g, pt: (i, 0, 0)),
                  pl.BlockSpec((1, 2 * IDX_HEADS, 1), lambda i, g, pt: (i, 0, 0))]
        + _page_specs(2, (PAGE_SIZE, IDX_DIM), slot, n_pages, pps),
        out_specs=pl.BlockSpec((1, 1, pps * PAGE_SIZE), lambda i, g, pt: (i, 0, g)),
    )
    return pl.pallas_call(
        functools.partial(_a_idx_kernel, pps=pps),
        grid_spec=grid_spec,
        out_shape=jax.ShapeDtypeStruct((db, 1, n_pages * PAGE_SIZE), F32),
        compiler_params=_cparams(("parallel", "arbitrary")),
        name="a_idx_scores",
    )(page_table.reshape(-1), qi, wi, *([pool_idx] * pps))


def _a_select_kernel(sp_ref, qi_ref, ki_ref, wi_ref, tri_ref, ma_ref, key_ref, *, topk, n_past):
    rows = sp_ref.shape[0]
    nch = n_past // KC
    for c in range(nch):
        key_ref[c] = _sortable(sp_ref[:, c * KC:(c + 1) * KC])
    d = jnp.sum(qi_ref[...] * ki_ref[...], axis=-1)
    s_new = jnp.sum(jnp.maximum(d, 0.0) * (wi_ref[...] * (IDX_HEADS ** -0.5)), axis=-1, keepdims=True)
    lane = lax.broadcasted_iota(I32, (rows, KC), 1)
    key_ref[nch] = _sortable(jnp.where(lane == 0, s_new, -jnp.inf))
    thr, cnt_thr = _kth_largest(key_ref, nch + 1, rows, topk)
    _write_select_mask(key_ref, ma_ref, nch + 1, rows, topk, thr, cnt_thr, tri_ref)


def _a_select(scores_past, qi, ki, wi, topk):
    db, n_past = scores_past.shape
    nch = n_past // KC
    args = (scores_past, qi, ki, wi, _tri(KC))
    return pl.pallas_call(
        functools.partial(_a_select_kernel, topk=topk, n_past=n_past),
        grid=(1,),
        in_specs=[pl.BlockSpec(a.shape, lambda i, nd=a.ndim: (0,) * nd) for a in args],
        out_specs=pl.BlockSpec((nch + 1, db, KC), lambda i: (0, 0, 0)),
        out_shape=jax.ShapeDtypeStruct((nch + 1, db, KC), F32),
        scratch_shapes=[pltpu.VMEM((nch + 1, db, KC), I32)],
        compiler_params=_cparams(("arbitrary",)),
        name="a_select",
    )(*args)


def _paged_attn_kernel(pt_ref, *refs, pps, n_q, n_kparts, v_part, has_mask, has_bias):
    g = pl.program_id(1)
    ng = pl.num_programs(1)
    it = iter(refs)
    q_refs = [next(it) for _ in range(n_kparts)]
    kn_refs = [next(it) for _ in range(n_kparts)]
    vn_ref = next(it)
    an_ref = next(it)
    bias_ref = next(it) if has_bias else None
    mask_ref = next(it) if has_mask else None
    k_pages = [[next(it) for _ in range(pps)] for _ in range(n_kparts)]
    v_pages = k_pages[v_part] if v_part >= 0 else [next(it) for _ in range(pps)]
    o_ref, m_ref, l_ref, acc_ref = next(it), next(it), next(it), next(it)

    @pl.when(g == 0)
    def _():
        s_new = an_ref[0]
        for q_ref, kn_ref in zip(q_refs, kn_refs):
            s_new = s_new + jnp.sum(q_ref[0].astype(F32) * kn_ref[0], axis=-1, keepdims=True)
        m_ref[...] = s_new
        l_ref[...] = jnp.ones_like(l_ref)
        acc_ref[...] = jnp.broadcast_to(vn_ref[0], acc_ref.shape)

    parts = []
    vb = []
    for p in range(pps):
        s = None
        for part in range(n_kparts):
            kb = k_pages[part][p][...].astype(BF16)
            if part == v_part:
                vb.append(kb)
            d = lax.dot_general(q_refs[part][0], kb, (((1,), (1,)), ((), ())), preferred_element_type=F32)
            s = d if s is None else s + d
        if v_part < 0:
            vb.append(v_pages[p][...].astype(BF16))
        parts.append(s)
    s = jnp.concatenate(parts, axis=-1)
    if has_bias:
        s = s + bias_ref[...]
    if has_mask:
        s = s + mask_ref[0]
    m = m_ref[...]
    m_new = jnp.maximum(m, jnp.max(s, axis=-1, keepdims=True))
    alpha = jnp.exp(m - m_new)
    p_all = jnp.exp(s - m_new)
    l_ref[...] = alpha * l_ref[...] + jnp.sum(p_all, axis=-1, keepdims=True)
    pb = p_all.astype(BF16)
    acc = alpha * acc_ref[...]
    for p in range(pps):
        acc = acc + jnp.dot(pb[:, p * PAGE_SIZE:(p + 1) * PAGE_SIZE], vb[p], preferred_element_type=F32)
    acc_ref[...] = acc
    m_ref[...] = m_new

    @pl.when(g == ng - 1)
    def _():
        o_ref[0] = acc_ref[...] / l_ref[...]


def _paged_attn(page_table, slot, q_parts, knew_parts, v_new, add_new, bias, mask, k_pools, v_pool, v_part):
    db, rows = q_parts[0].shape[:2]
    n_pages = page_table.shape[1]
    pps = PAGES_PER_STEP
    dv = v_new.shape[-1]
    n_kparts = len(q_parts)
    in_specs, args = [], []
    for q in q_parts:
        in_specs.append(pl.BlockSpec((1, rows, q.shape[-1]), lambda i, g, pt: (i, 0, 0)))
        args.append(q)
    for kn in knew_parts + [v_new]:
        in_specs.append(pl.BlockSpec((1, 1, kn.shape[-1]), lambda i, g, pt: (i, 0, 0)))
        args.append(kn)
    in_specs.append(pl.BlockSpec((1, rows, 1), lambda i, g, pt: (i, 0, 0)))
    args.append(add_new)
    if bias is not None:
        in_specs.append(pl.BlockSpec((rows, pps * PAGE_SIZE), lambda i, g, pt: (0, g)))
        args.append(bias)
    if mask is not None:
        in_specs.append(pl.BlockSpec((1, 1, pps * PAGE_SIZE), lambda i, g, pt: (i, 0, g)))
        args.append(mask)
    for pool in k_pools:
        in_specs += _page_specs(2, (PAGE_SIZE, pool.shape[-1]), slot, n_pages, pps)
        args += [pool] * pps
    if v_part < 0:
        in_specs += _page_specs(2, (PAGE_SIZE, dv), slot, n_pages, pps)
        args += [v_pool] * pps
    grid_spec = pltpu.PrefetchScalarGridSpec(
        num_scalar_prefetch=1,
        grid=(db, n_pages // pps),
        in_specs=in_specs,
        out_specs=pl.BlockSpec((1, rows, dv), lambda i, g, pt: (i, 0, 0)),
        scratch_shapes=[pltpu.VMEM((rows, 1), F32), pltpu.VMEM((rows, 1), F32), pltpu.VMEM((rows, dv), F32)],
    )
    return pl.pallas_call(
        functools.partial(_paged_attn_kernel, pps=pps, n_q=rows, n_kparts=n_kparts, v_part=v_part,
                          has_mask=mask is not None, has_bias=bias is not None),
        grid_spec=grid_spec,
        out_shape=jax.ShapeDtypeStruct((db, rows, dv), F32),
        compiler_params=_cparams(("parallel", "arbitrary")),
        name="paged_attn",
    )(page_table.reshape(-1), *args)


def _c_post_kernel(o_ref, wuv_ref, wo_ref, x_ref, y_ref):
    parts = []
    for h in range(C_HEADS):
        parts.append(jnp.dot(o_ref[h].astype(BF16), wuv_ref[h], preferred_element_type=F32))
    o = jnp.concatenate(parts, axis=-1).astype(BF16)
    y_ref[...] = x_ref[...] + jnp.dot(o, wo_ref[...], preferred_element_type=F32)


def _c_post(o_lat, w_uv, w_o, x):
    args = (o_lat, w_uv, w_o, x)
    return pl.pallas_call(
        _c_post_kernel,
        grid=(1,),
        in_specs=[pl.BlockSpec(a.shape, lambda i, nd=a.ndim: (0,) * nd) for a in args],
        out_specs=pl.BlockSpec(x.shape, lambda i: (0, 0)),
        out_shape=jax.ShapeDtypeStruct(x.shape, F32),
        compiler_params=_cparams(("arbitrary",)),
        name="c_post",
    )(*args)


def _blockdiag_q(q, n_blocks):
    db, nb, g, d = q.shape
    eye = jnp.eye(nb, dtype=q.dtype)
    return jnp.einsum('bkgd,kc->bkgcd', q, eye).reshape(db, nb * g, nb * d)


def _diag_blocks(o, n_blocks, g):
    db, rows, cols = o.shape
    e = cols // n_blocks
    o = o.reshape(db, n_blocks, g, n_blocks, e)
    return jnp.stack([o[:, k, :, k] for k in range(n_blocks)], axis=1)


def _mixer_a(xp, xs, g, w_in, w_o, rel_table, pool_k, pool_v, pool_i, slot, page_table):
    b, s, d = xp.shape
    db = xs.shape[0]
    ws = [w.astype(BF16) for w in jnp.split(w_in, np.cumsum(A_SIZES)[:-1].tolist(), axis=-1)]
    ws = ws[:4] + [jnp.pad(jnp.concatenate(ws[4:], axis=1), ((0, 0), (0, LANE - IDX_DIM - IDX_HEADS)))]
    dts = [BF16, F32, F32, BF16, F32]
    scales = [HEAD_DIM ** -0.5, 1.0, 1.0, IDX_DIM ** -0.5, 1.0]
    wo = w_o.astype(BF16)
    cols = list(range(A_HEADS))

    def project(x2, tm):
        q, k, v, qi, kw = _norm_proj(x2, g, ws, dts, scales, tm)
        return q, k, v, qi, kw[:, :IDX_DIM], kw[:, IDX_DIM:IDX_DIM + IDX_HEADS]

    q, k, v, qi, ki, wi = project(xp.reshape(b * s, d), 512)
    nq, nc = s // TQ, s // KC
    topk = min(TOPK_MAX, s // 4)
    tiles, far = _bias_tiles(rel_table, cols)
    tiles = tiles.reshape(A_KV_HEADS, A_GROUP, 2, TQ, KC).transpose(0, 2, 1, 3, 4).reshape(A_KV_HEADS, 2, A_GROUP * TQ, KC)
    q_l = q.reshape(b, nq, TQ, A_KV_HEADS, A_GROUP, HEAD_DIM).transpose(0, 1, 3, 4, 2, 5)
    q_l = q_l.reshape(b, nq, A_KV_HEADS, A_GROUP * TQ, HEAD_DIM)
    kb = k.astype(BF16).reshape(b, nc, KC, A_KV_HEADS, HEAD_DIM)
    kt = kb.transpose(0, 3, 1, 4, 2)
    vl = v.astype(BF16).reshape(b, nc, KC, A_KV_HEADS, HEAD_DIM).transpose(0, 3, 1, 2, 4)
    qi_l = qi.reshape(b, nq, TQ, IDX_HEADS, IDX_DIM).transpose(0, 1, 3, 2, 4).reshape(b, nq, IDX_HEADS * TQ, IDX_DIM)
    kit = ki.astype(BF16).reshape(b, nc, KC, IDX_DIM).transpose(0, 1, 3, 2)
    attn = _a_prompt_attn(far, qi_l, wi.reshape(b, s, IDX_HEADS), kit, q_l, kt, vl, tiles, topk)
    yp = _out_proj(attn.reshape(b * s, -1), wo, xp.reshape(b * s, d), 512).reshape(b, s, d)
    rows_p = (k.reshape(b, s, A_KV_HEADS, HEAD_DIM), v.reshape(b, s, A_KV_HEADS, HEAD_DIM), ki.reshape(b, s, IDX_DIM))
    n_pages = page_table.shape[1]
    past = n_pages * PAGE_SIZE
    q, k, v, qi, ki, wi = project(xs.reshape(db, d), db)
    qi3 = qi.reshape(db, IDX_HEADS, IDX_DIM)
    sc = _a_idx_scores(page_table, jnp.pad(qi3, ((0, 0), (0, IDX_HEADS), (0, 0))),
                       jnp.pad(wi.reshape(db, IDX_HEADS, 1), ((0, 0), (0, IDX_HEADS), (0, 0))), pool_i, slot)
    topk_s = min(TOPK_MAX, (past + 1) // 4)
    ma = _a_select(sc.reshape(db, past), qi3.astype(F32), ki.reshape(db, 1, IDX_DIM), wi, topk_s)
    mask = ma[:past // KC].transpose(1, 0, 2).reshape(db, 1, past)
    dist = past - np.arange(past)
    tab = rel_table.astype(F32)
    bias = tab[_rel_bucket_np(dist)].T
    add_new = tab[0][None, :, None] + ma[past // KC, :, 0][:, None, None]
    qm = _blockdiag_q(q.reshape(db, A_KV_HEADS, A_GROUP, HEAD_DIM), A_KV_HEADS)
    o = _paged_attn(page_table, slot, [qm], [k.reshape(db, 1, -1)], v.reshape(db, 1, -1), add_new, bias, mask,
                    [pool_k.reshape(pool_k.shape[:3] + (-1,))], pool_v.reshape(pool_v.shape[:3] + (-1,)), -1)
    o = _diag_blocks(o, A_KV_HEADS, A_GROUP).reshape(db, A_HEADS * HEAD_DIM)
    ys = _out_proj(o, wo, xs.reshape(db, d), db).reshape(db, 1, d)
    rows_s = (k.reshape(db, 1, A_KV_HEADS, HEAD_DIM), v.reshape(db, 1, A_KV_HEADS, HEAD_DIM), ki.reshape(db, 1, IDX_DIM))
    return yp, ys, rows_p, rows_s


def _mixer_b(xp, xs, g, w_in, lam_p, subln, w_o, rel_table, lam_init, pool_k, pool_v, slot, page_table):
    b, s, d = xp.shape
    db = xs.shape[0]
    hd = 2 * HEAD_DIM
    ws = [w.astype(BF16) for w in jnp.split(w_in, np.cumsum(B_SIZES)[:-1].tolist(), axis=-1)]
    dts = [BF16, F32, F32]
    scales = [HEAD_DIM ** -0.5, 1.0, 1.0]
    wo = w_o.astype(BF16)
    ngrp = B_KV_HEADS * 2
    cols = [2 * (kv * B_GROUP + gg) + j for kv in range(B_KV_HEADS) for j in range(2) for gg in range(B_GROUP)]
    q, k, v = _norm_proj(xp.reshape(b * s, d), g, ws, dts, scales, 512)
    nq, nc = s // TQ, s // KC
    tiles, far = _bias_tiles(rel_table, cols)
    tiles = tiles.reshape(ngrp, B_GROUP, 2, TQ, KC).transpose(0, 2, 1, 3, 4).reshape(ngrp, 2, B_GROUP * TQ, KC)
    q_l = q.reshape(b, nq, TQ, B_KV_HEADS, B_GROUP, 2, HEAD_DIM).transpose(0, 1, 3, 5, 4, 2, 6)
    q_l = q_l.reshape(b, nq, ngrp, B_GROUP * TQ, HEAD_DIM)
    kt = k.astype(BF16).reshape(b, nc, KC, ngrp, HEAD_DIM).transpose(0, 3, 1, 4, 2)
    vl = v.astype(BF16).reshape(b, nc, KC, B_KV_HEADS, hd).transpose(0, 3, 1, 2, 4)
    o0, o1 = _b_prompt_attn(far, q_l, kt, vl, tiles)
    yp = _b_post(o0.reshape(b * s, -1), o1.reshape(b * s, -1), lam_p, subln, wo, xp.reshape(b * s, d),
                 lam_init, 512).reshape(b, s, d)
    rows_p = (k.reshape(b, s, B_KV_HEADS, 2, HEAD_DIM), v.reshape(b, s, B_KV_HEADS, hd))
    n_pages = page_table.shape[1]
    past = n_pages * PAGE_SIZE
    q, k, v = _norm_proj(xs.reshape(db, d), g, ws, dts, scales, db)
    tab = rel_table.astype(F32)
    colr = np.asarray(cols).reshape(B_KV_HEADS, 2, B_GROUP).transpose(0, 2, 1).reshape(-1)
    bias = tab[_rel_bucket_np(past - np.arange(past))][:, colr].T
    add_new = jnp.broadcast_to(tab[0][colr][None, :, None], (db, len(cols), 1))
    q5 = q.reshape(db, B_KV_HEADS, B_GROUP, 2, HEAD_DIM)
    eye = jnp.eye(ngrp, dtype=q.dtype).reshape(B_KV_HEADS, 2, B_KV_HEADS, 2)
    qm = jnp.einsum('bkgjd,kjcm->bkgjcmd', q5, eye).reshape(db, ngrp * B_GROUP, ngrp * HEAD_DIM)
    o = _paged_attn(page_table, slot, [qm], [k.reshape(db, 1, -1)], v.reshape(db, 1, -1), add_new, bias, None,
                    [pool_k.reshape(pool_k.shape[:3] + (-1,))], pool_v.reshape(pool_v.shape[:3] + (-1,)), -1)
    o = _diag_blocks(o, B_KV_HEADS, B_GROUP * 2).reshape(db, B_KV_HEADS, B_GROUP, 2, hd)
    o0 = o[:, :, :, 0].reshape(db, B_HEADS * hd)
    o1 = o[:, :, :, 1].reshape(db, B_HEADS * hd)
    ys = _b_post(o0, o1, lam_p, subln, wo, xs.reshape(db, d), lam_init, db).reshape(db, 1, d)
    rows_s = (k.reshape(db, 1, B_KV_HEADS, 2, HEAD_DIM), v.reshape(db, 1, B_KV_HEADS, hd))
    return yp, ys, rows_p, rows_s


def _mixer_c(xp, xs, g, w_down, q_norm, w_uq, kv_norm, w_ukv, w_o, pool_ckv, pool_kpe, slot, page_table):
    b, s, d = xp.shape
    db = xs.shape[0]
    n_pages = page_table.shape[1]
    past = n_pages * PAGE_SIZE
    wd = jnp.pad(w_down.astype(BF16), ((0, 0), (0, Q_LORA + KV_LORA + LANE - w_down.shape[1])))
    wq = w_uq.astype(BF16).reshape(Q_LORA, C_HEADS, C_QK)
    wq_perm = jnp.concatenate([wq[:, :, :NOPE_DIM].reshape(Q_LORA, -1), wq[:, :, NOPE_DIM:].reshape(Q_LORA, -1)], axis=1)
    wkv = w_ukv.astype(BF16).reshape(KV_LORA, C_HEADS, NOPE_DIM + V_DIM)
    w_uk = wkv[:, :, :NOPE_DIM].reshape(KV_LORA, -1)
    w_uv = wkv[:, :, NOPE_DIM:].reshape(KV_LORA, -1)
    wo = w_o.astype(BF16)
    tabs = _rope_tables(np.arange(s))
    qn, qp, ckv, kpe, kn, v = _c_proj(xp.reshape(b * s, d), g, wd, q_norm, wq_perm, kv_norm, w_uk, w_uv, *tabs, 512, s)
    nq, nc = s // TQ, s // KC
    q_l = jnp.concatenate([qn.reshape(b, s, C_HEADS, NOPE_DIM), qp.reshape(b, s, C_HEADS, ROPE_DIM)], axis=-1)
    q_l = q_l.reshape(b, nq, TQ, C_HEADS, C_QK).transpose(0, 1, 3, 2, 4)
    kpe_b = jnp.broadcast_to(kpe.astype(BF16).reshape(b, s, 1, ROPE_DIM), (b, s, C_HEADS, ROPE_DIM))
    k_l = jnp.concatenate([kn.reshape(b, s, C_HEADS, NOPE_DIM), kpe_b], axis=-1)
    kt = k_l.reshape(b, nc, KC, C_HEADS, C_QK).transpose(0, 3, 1, 4, 2)
    vl = v.reshape(b, nc, KC, C_HEADS, V_DIM).transpose(0, 3, 1, 2, 4)
    i = np.arange(TQ)[:, None]
    j = np.arange(KC)[None, :]
    tiles = jnp.asarray(np.stack([np.zeros((TQ, KC), np.float32),
                                  np.where(i >= j, 0.0, MASKED).astype(np.float32)]))
    attn = _c_prompt_attn(q_l, kt, vl, tiles)
    yp = _out_proj(attn.reshape(b * s, -1), wo, xp.reshape(b * s, d), 512).reshape(b, s, d)
    rows_p = (ckv.reshape(b, s, KV_LORA), kpe.reshape(b, s, ROPE_DIM))
    tabs = _rope_tables(past + np.arange(1))
    tabs = tuple(jnp.broadcast_to(t, (db, t.shape[1])) for t in tabs)
    qn, qp, ckv, kpe, _, _ = _c_proj(xs.reshape(db, d), g, wd, q_norm, wq_perm, kv_norm, w_uk, w_uv, *tabs, db, db)
    o_lat = _c_sample_attend(qn, qp, ckv, kpe, wkv, pool_ckv, pool_kpe, slot, page_table)
    w_uv_h = wkv[:, :, NOPE_DIM:].transpose(1, 0, 2)
    ys = _c_post(o_lat.transpose(1, 0, 2), w_uv_h, wo, xs.reshape(db, d)).reshape(db, 1, d)
    rows_s = (ckv.reshape(db, 1, KV_LORA), kpe.reshape(db, 1, ROPE_DIM))
    return yp, ys, rows_p, rows_s


def _q_lat_kernel(qn_ref, wuk_ref, o_ref):
    for h in range(C_HEADS):
        o_ref[h] = lax.dot_general(qn_ref[:, h * NOPE_DIM:(h + 1) * NOPE_DIM], wuk_ref[h],
                                   (((1,), (1,)), ((), ())), preferred_element_type=F32).astype(o_ref.dtype)


def _c_sample_attend(qn, qp, ckv, kpe, wkv, pool_ckv, pool_kpe, slot, page_table):
    db = qn.shape[0]
    w_uk_h = wkv[:, :, :NOPE_DIM].transpose(1, 0, 2)
    args = (qn, w_uk_h)
    q_lat = pl.pallas_call(
        _q_lat_kernel,
        grid=(1,),
        in_specs=[pl.BlockSpec(a.shape, lambda i, nd=a.ndim: (0,) * nd) for a in args],
        out_specs=pl.BlockSpec((C_HEADS, db, KV_LORA), lambda i: (0, 0, 0)),
        out_shape=jax.ShapeDtypeStruct((C_HEADS, db, KV_LORA), BF16),
        compiler_params=_cparams(("arbitrary",)),
        name="c_q_lat",
    )(*args).transpose(1, 0, 2)
    return _paged_attn(page_table, slot, [q_lat, qp.reshape(db, C_HEADS, ROPE_DIM)],
                       [ckv.reshape(db, 1, KV_LORA), kpe.reshape(db, 1, ROPE_DIM)], ckv.reshape(db, 1, KV_LORA),
                       jnp.zeros((db, C_HEADS, 1), F32), None, None, [pool_ckv, pool_kpe], None, 0)


def kernel(x_prompt, x_sample, cache_a_k, cache_a_v, cache_a_idx, cache_b_k, cache_b_v, cache_c_ckv, cache_c_kpe, state_conv, page_table, rel_table, norm_mix, norm_ffn, norm_final, a_w_in, a_w_o, b_w_in, b_lambda, b_subln, b_w_o, c_w_down, c_q_norm, c_w_uq, c_kv_norm, c_w_ukv, c_w_o, ffn_w_up, ffn_conv_w, ffn_conv_b, ffn_w_down):
    depth = norm_mix.shape[0]
    db = x_sample.shape[0]
    xp, xs = x_prompt, x_sample
    rows_p = {0: [], 1: [], 2: []}
    rows_s = {0: [], 1: [], 2: []}
    conv_p, conv_s = [], []
    for i in range(depth):
        kind, slot = i % N_MIXERS, i // N_MIXERS
        if kind == 0:
            yp, ys, rp, rs = _mixer_a(xp, xs, norm_mix[i], a_w_in[slot], a_w_o[slot], rel_table,
                                      cache_a_k, cache_a_v, cache_a_idx, slot, page_table)
        elif kind == 1:
            lam_init = 0.8 - 0.6 * math.exp(-0.3 * i)
            yp, ys, rp, rs = _mixer_b(xp, xs, norm_mix[i], b_w_in[slot], b_lambda[slot], b_subln[slot], b_w_o[slot],
                                      rel_table, lam_init, cache_b_k, cache_b_v, slot, page_table)
        else:
            yp, ys, rp, rs = _mixer_c(xp, xs, norm_mix[i], c_w_down[slot], c_q_norm[slot], c_w_uq[slot],
                                      c_kv_norm[slot], c_w_ukv[slot], c_w_o[slot],
                                      cache_c_ckv, cache_c_kpe, slot, page_table)
        rows_p[kind].append(rp)
        rows_s[kind].append(rs)
        xp, xs = yp, ys
        fw = _ffn_weights(ffn_w_up[i], ffn_conv_w[i], ffn_conv_b[i], ffn_w_down[i])
        last = i == depth - 1
        xp, cp = _ffn_prompt(xp, norm_ffn[i], fw, norm_final, last, 512)
        xs2, cs = _ffn_sample(xs.reshape(db, -1), norm_ffn[i], fw, norm_final, last, state_conv[i])
        xs = xs2.reshape(xs.shape)
        conv_p.append(cp)
        conv_s.append(cs)

    def stack(rows, j):
        return jnp.stack([r[j] for r in rows])

    a_p, b_p, c_p = rows_p[0], rows_p[1], rows_p[2]
    a_s, b_s, c_s = rows_s[0], rows_s[1], rows_s[2]
    return (xp, xs,
            stack(a_p, 0), stack(a_p, 1), stack(a_p, 2), stack(b_p, 0), stack(b_p, 1),
            stack(c_p, 0), stack(c_p, 1), jnp.stack(conv_p),
            stack(a_s, 0), stack(a_s, 1), stack(a_s, 2), stack(b_s, 0), stack(b_s, 1),
            stack(c_s, 0), stack(c_s, 1), jnp.stack(conv_s))
```

```python
import functools
import math

import numpy as np
import jax
import jax.numpy as jnp
from jax import lax
from jax.experimental import pallas as pl
from jax.experimental.pallas import tpu as pltpu

F32 = jnp.float32
BF16 = jnp.bfloat16
I32 = jnp.int32

N_MIXERS = 3
HEAD_DIM = 64
PAGE_SIZE = 128
A_HEADS, A_KV_HEADS = 16, 4
A_GROUP = A_HEADS // A_KV_HEADS
IDX_HEADS, IDX_DIM = 8, 64
TOPK_MAX = 256
A_SIZES = (A_HEADS * HEAD_DIM, A_KV_HEADS * HEAD_DIM, A_KV_HEADS * HEAD_DIM,
           IDX_HEADS * IDX_DIM, IDX_DIM, IDX_HEADS)
B_HEADS, B_KV_HEADS = 8, 4
B_GROUP = B_HEADS // B_KV_HEADS
B_SIZES = (B_HEADS * 2 * HEAD_DIM, B_KV_HEADS * 2 * HEAD_DIM, B_KV_HEADS * 2 * HEAD_DIM)
C_HEADS = 16
Q_LORA, KV_LORA, NOPE_DIM, ROPE_DIM, V_DIM = 384, 256, 64, 32, 64
C_QK = NOPE_DIM + ROPE_DIM
C_SCALE = C_QK ** -0.5
ROPE_THETA = 10000.0
REL_BUCKETS, REL_MAX_DIST, N_BIAS = 32, 128, 16
CONV_W = 3
EPS = 1e-6

LANE = 128
TQ = 256
KC = 256
FFN_CK = 256
MAX_PAGES_PER_STEP = 32
PAGE_VMEM_BUDGET = 20 * 1024 * 1024
IDX_PAGES_PER_STEP = 32
MASKED = -1e30
INT_MIN = -2 ** 31
VMEM_LIMIT = 56 * 1024 * 1024
COL_BIAS_HI, COL_BIAS_LO, COL_SHIFT = HEAD_DIM, HEAD_DIM + 1, HEAD_DIM + 2
C_COL_SHIFT = C_QK


def _cparams(sem):
    return pltpu.CompilerParams(dimension_semantics=sem, vmem_limit_bytes=VMEM_LIMIT)


def _rms(x, g):
    return x * lax.rsqrt(jnp.mean(x * x, axis=-1, keepdims=True) + EPS) * g


def _rel_bucket_np(dist):
    n = np.maximum(np.asarray(dist, np.int64), 0)
    max_exact = REL_BUCKETS // 2
    nf = np.maximum(n, 1).astype(np.float32)
    large = max_exact + (np.log(nf / np.float32(max_exact)) / np.float32(math.log(REL_MAX_DIST / max_exact))
                         * np.float32(REL_BUCKETS - max_exact)).astype(np.int32)
    return np.where(n < max_exact, n, np.minimum(large, REL_BUCKETS - 1)).astype(np.int32)


def _bias_lookup(table, dist):
    bucket = _rel_bucket_np(dist)
    onehot = jnp.asarray(bucket.reshape(-1, 1) == np.arange(REL_BUCKETS)[None, :], F32)
    out = jnp.dot(onehot, table.astype(F32), precision=lax.Precision.HIGHEST)
    return out.reshape(bucket.shape + (table.shape[1],))


def _rep(x, n):
    return x if n == LANE else jnp.concatenate([x] * (n // LANE), axis=1)


def _norm_proj_kernel(x_ref, g_ref, *refs, n_out, scales):
    w_refs, o_refs = refs[:n_out], refs[n_out:]
    hb = _rms(x_ref[...], g_ref[...]).astype(BF16)
    for w_ref, o_ref, s in zip(w_refs, o_refs, scales):
        y = jnp.dot(hb, w_ref[...], preferred_element_type=F32)
        if s != 1.0:
            y = y * s
        o_ref[...] = y.astype(o_ref.dtype)


def _norm_proj(x, g, ws, dtypes, scales, tm):
    m, d = x.shape
    n_out = len(ws)
    return pl.pallas_call(
        functools.partial(_norm_proj_kernel, n_out=n_out, scales=tuple(scales)),
        grid=(m // tm,),
        in_specs=[pl.BlockSpec((tm, d), lambda i: (i, 0)), pl.BlockSpec((1, d), lambda i: (0, 0))]
        + [pl.BlockSpec(w.shape, lambda i: (0, 0)) for w in ws],
        out_specs=[pl.BlockSpec((tm, w.shape[1]), lambda i: (i, 0)) for w in ws],
        out_shape=[jax.ShapeDtypeStruct((m, w.shape[1]), dt) for w, dt in zip(ws, dtypes)],
        compiler_params=_cparams(("parallel",)),
        name="norm_proj",
    )(x, g.reshape(1, d), *ws)


def _out_proj_kernel(a_ref, w_ref, x_ref, o_ref):
    o_ref[...] = x_ref[...] + jnp.dot(a_ref[...].astype(BF16), w_ref[...], preferred_element_type=F32)


def _out_proj(a, w, x, tm):
    m, k = a.shape
    d = w.shape[1]
    return pl.pallas_call(
        _out_proj_kernel,
        grid=(m // tm,),
        in_specs=[pl.BlockSpec((tm, k), lambda i: (i, 0)), pl.BlockSpec((k, d), lambda i: (0, 0)),
                  pl.BlockSpec((tm, d), lambda i: (i, 0))],
        out_specs=pl.BlockSpec((tm, d), lambda i: (i, 0)),
        out_shape=jax.ShapeDtypeStruct((m, d), F32),
        compiler_params=_cparams(("parallel",)),
        name="out_proj",
    )(a, w, x)


def _b_post_kernel(o0_ref, o1_ref, lam_ref, sub_ref, w_ref, x_ref, o_ref, *, lam_init, n_heads):
    lp = lam_ref[...]
    lam = (jnp.exp(jnp.sum(lp[0:1] * lp[1:2], axis=-1, keepdims=True))
           - jnp.exp(jnp.sum(lp[2:3] * lp[3:4], axis=-1, keepdims=True)) + lam_init)
    d = o0_ref[...] - lam * o1_ref[...]
    hd = 2 * HEAD_DIM
    parts = []
    for h in range(n_heads):
        dh = d[:, h * hd:(h + 1) * hd]
        parts.append(_rms(dh, sub_ref[...]) * (1.0 - lam_init))
    dn = jnp.concatenate(parts, axis=-1).astype(BF16)
    o_ref[...] = x_ref[...] + jnp.dot(dn, w_ref[...], preferred_element_type=F32)


def _b_post(o0, o1, lam_p, subln, w_o, x, lam_init, tm):
    m, k = o0.shape
    d = w_o.shape[1]
    return pl.pallas_call(
        functools.partial(_b_post_kernel, lam_init=lam_init, n_heads=B_HEADS),
        grid=(m // tm,),
        in_specs=[pl.BlockSpec((tm, k), lambda i: (i, 0)), pl.BlockSpec((tm, k), lambda i: (i, 0)),
                  pl.BlockSpec(lam_p.shape, lambda i: (0, 0)), pl.BlockSpec((1, 2 * HEAD_DIM), lambda i: (0, 0)),
                  pl.BlockSpec((k, d), lambda i: (0, 0)), pl.BlockSpec((tm, d), lambda i: (i, 0))],
        out_specs=pl.BlockSpec((tm, d), lambda i: (i, 0)),
        out_shape=jax.ShapeDtypeStruct((m, d), F32),
        compiler_params=_cparams(("parallel",)),
        name="b_post",
    )(o0, o1, lam_p, subln.reshape(1, -1), w_o, x)


def _pad_heads(w, n_heads, dk):
    d = w.shape[0]
    return jnp.pad(w.reshape(d, n_heads, dk), ((0, 0), (0, 0), (0, LANE - dk))).reshape(d, n_heads * LANE)


def _a_proj_kernel(x_ref, g_ref, wq_ref, wk_ref, wv_ref, wqi_ref, wkw_ref, wki_ref,
                   qh_ref, qi_ref, kt_ref, vt_ref, kw_ref, kit_ref):
    hb = _rms(x_ref[0], g_ref[...]).astype(BF16)
    yq = jnp.dot(hb, wq_ref[...], preferred_element_type=F32)
    for h in range(A_HEADS):
        qh_ref[0, 0, h] = yq[:, h * LANE:(h + 1) * LANE].astype(BF16)
    yqi = jnp.dot(hb, wqi_ref[...], preferred_element_type=F32)
    for h in range(IDX_HEADS):
        qi_ref[0, 0, h * TQ:(h + 1) * TQ, :] = yqi[:, h * LANE:(h + 1) * LANE].astype(BF16)
    kt_ref[0] = lax.dot_general(wk_ref[...], hb, _NT, preferred_element_type=F32)
    vt_ref[0] = lax.dot_general(wv_ref[...], hb, _NT, preferred_element_type=F32)
    kw_ref[0] = jnp.dot(hb, wkw_ref[...], preferred_element_type=F32)
    kit_ref[0] = lax.dot_general(wki_ref[...], hb, _NT, preferred_element_type=F32)


def _a_proj(x, g, wq, wk_t, wv_t, wqi, wkw, wki_t):
    b, s, d = x.shape
    nq = s // TQ
    kvd = A_KV_HEADS * HEAD_DIM
    full = lambda a: pl.BlockSpec(a.shape, lambda i, j: (0, 0))
    return pl.pallas_call(
        _a_proj_kernel,
        grid=(b, nq),
        in_specs=[pl.BlockSpec((1, TQ, d), lambda i, j: (i, j, 0)), pl.BlockSpec((1, d), lambda i, j: (0, 0)),
                  full(wq), full(wk_t), full(wv_t), full(wqi), full(wkw), full(wki_t)],
        out_specs=[pl.BlockSpec((1, 1, A_HEADS, TQ, LANE), lambda i, j: (i, j, 0, 0, 0)),
                   pl.BlockSpec((1, 1, IDX_HEADS * TQ, LANE), lambda i, j: (i, j, 0, 0)),
                   pl.BlockSpec((1, kvd, TQ), lambda i, j: (i, 0, j)),
                   pl.BlockSpec((1, kvd, TQ), lambda i, j: (i, 0, j)),
                   pl.BlockSpec((1, TQ, LANE), lambda i, j: (i, j, 0)),
                   pl.BlockSpec((1, IDX_DIM, TQ), lambda i, j: (i, 0, j))],
        out_shape=[jax.ShapeDtypeStruct((b, nq, A_HEADS, TQ, LANE), BF16),
                   jax.ShapeDtypeStruct((b, nq, IDX_HEADS * TQ, LANE), BF16),
                   jax.ShapeDtypeStruct((b, kvd, s), F32), jax.ShapeDtypeStruct((b, kvd, s), F32),
                   jax.ShapeDtypeStruct((b, s, LANE), F32), jax.ShapeDtypeStruct((b, IDX_DIM, s), F32)],
        compiler_params=_cparams(("parallel", "parallel")),
        name="a_proj",
    )(x, g.reshape(1, d), wq, wk_t, wv_t, wqi, wkw, wki_t)


def _b_proj_kernel(x_ref, g_ref, wq_ref, wk_ref, wv_ref, qh_ref, kt_ref, v_ref):
    hb = _rms(x_ref[0], g_ref[...]).astype(BF16)
    yq = jnp.dot(hb, wq_ref[...], preferred_element_type=F32)
    for grp in range(B_KV_HEADS * 2):
        for gg in range(B_GROUP):
            blk = grp * B_GROUP + gg
            qh_ref[0, 0, grp, gg * TQ:(gg + 1) * TQ, :] = yq[:, blk * LANE:(blk + 1) * LANE].astype(BF16)
    kt_ref[0] = lax.dot_general(wk_ref[...], hb, _NT, preferred_element_type=F32)
    v_ref[0] = jnp.dot(hb, wv_ref[...], preferred_element_type=F32)


def _b_proj(x, g, wq, wk, wv):
    b, s, d = x.shape
    nq = s // TQ
    ngrp = B_KV_HEADS * 2
    kd = B_SIZES[1]
    full = lambda a: pl.BlockSpec(a.shape, lambda i, j: (0, 0))
    return pl.pallas_call(
        _b_proj_kernel,
        grid=(b, nq),
        in_specs=[pl.BlockSpec((1, TQ, d), lambda i, j: (i, j, 0)), pl.BlockSpec((1, d), lambda i, j: (0, 0)),
                  full(wq), full(wk), full(wv)],
        out_specs=[pl.BlockSpec((1, 1, ngrp, B_GROUP * TQ, LANE), lambda i, j: (i, j, 0, 0, 0)),
                   pl.BlockSpec((1, kd, TQ), lambda i, j: (i, 0, j)),
                   pl.BlockSpec((1, TQ, kd), lambda i, j: (i, j, 0))],
        out_shape=[jax.ShapeDtypeStruct((b, nq, ngrp, B_GROUP * TQ, LANE), BF16),
                   jax.ShapeDtypeStruct((b, kd, s), F32), jax.ShapeDtypeStruct((b, s, kd), F32)],
        compiler_params=_cparams(("parallel", "parallel")),
        name="b_proj",
    )(x, g.reshape(1, d), wq, wk, wv)


def _swap_halves(x, half):
    n = x.shape[-1]
    lane = lax.broadcasted_iota(I32, x.shape, x.ndim - 1)
    fwd = pltpu.roll(x, n - half, x.ndim - 1)
    bwd = pltpu.roll(x, half, x.ndim - 1)
    return jnp.where((lane // half) % 2 == 0, fwd, bwd)


def _c_proj_kernel(x_ref, g_ref, wd_ref, qn_ref, wuq_ref, kvn_ref, wuk_ref, wuv_ref,
                   cq_ref, sq_ref, ck_ref, sk_ref, q_o, k_o, v_o, ckv_o, kpet_o):
    hb = _rms(x_ref[0], g_ref[...]).astype(BF16)
    down = jnp.dot(hb, wd_ref[...], preferred_element_type=F32)
    cq = down[:, :Q_LORA]
    ckv = down[:, Q_LORA:Q_LORA + KV_LORA]
    kp = down[:, Q_LORA + KV_LORA:]
    qf = jnp.dot(_rms(cq, qn_ref[...]).astype(BF16), wuq_ref[...], preferred_element_type=F32)
    half = ROPE_DIM // 2
    cq_t, sq_t = cq_ref[...], sq_ref[...]
    for h in range(C_HEADS):
        x = qf[:, h * LANE:(h + 1) * LANE]
        x = x * cq_t + _swap_halves(x, half) * sq_t
        q_o[0, :, h * LANE:(h + 1) * LANE] = (x * C_SCALE).astype(BF16)
    ckv_n = _rms(ckv, kvn_ref[...])
    ckv_o[0] = ckv_n
    kp = kp * ck_ref[...] + _swap_halves(kp, half) * sk_ref[...]
    kpet_o[0] = kp.T[:ROPE_DIM]
    lane = lax.broadcasted_iota(I32, kp.shape, 1)
    k_tail = pltpu.roll(kp, NOPE_DIM, 1) + jnp.where(lane == C_COL_SHIFT, 1.0, 0.0)
    v_tail = jnp.where(lane == V_DIM, 1.0, 0.0)
    cb = ckv_n.astype(BF16)
    kn = jnp.dot(cb, wuk_ref[...], preferred_element_type=F32)
    vv = jnp.dot(cb, wuv_ref[...], preferred_element_type=F32)
    for h in range(C_HEADS):
        k_o[0, :, h * LANE:(h + 1) * LANE] = (kn[:, h * LANE:(h + 1) * LANE] + k_tail).astype(BF16)
        v_o[0, :, h * LANE:(h + 1) * LANE] = (vv[:, h * LANE:(h + 1) * LANE] + v_tail).astype(BF16)


def _c_proj(x, g, w_down, q_norm, w_uq_aug, kv_norm, w_uk_aug, w_uv_aug, tabs, tm):
    b, s, d = x.shape
    hw = C_HEADS * LANE
    full = lambda a: pl.BlockSpec(a.shape, lambda i, j: (0, 0))
    rows = lambda n: pl.BlockSpec((1, tm, n), lambda i, j: (i, j, 0))
    pos = pl.BlockSpec((tm, LANE), lambda i, j: (j, 0))
    qn2, kvn2 = q_norm.reshape(1, -1), kv_norm.reshape(1, -1)
    return pl.pallas_call(
        _c_proj_kernel,
        grid=(b, s // tm),
        in_specs=[rows(d), pl.BlockSpec((1, d), lambda i, j: (0, 0)), full(w_down), full(qn2), full(w_uq_aug),
                  full(kvn2), full(w_uk_aug), full(w_uv_aug), pos, pos, pos, pos],
        out_specs=[rows(hw), rows(hw), rows(hw), rows(KV_LORA),
                   pl.BlockSpec((1, ROPE_DIM, tm), lambda i, j: (i, 0, j))],
        out_shape=[jax.ShapeDtypeStruct((b, s, hw), BF16)] * 3
        + [jax.ShapeDtypeStruct((b, s, KV_LORA), F32), jax.ShapeDtypeStruct((b, ROPE_DIM, s), F32)],
        compiler_params=_cparams(("parallel", "parallel")),
        name="c_proj",
    )(x, g.reshape(1, d), w_down, qn2, w_uq_aug, kvn2, w_uk_aug, w_uv_aug, *tabs)


def _rope_tables(positions):
    half = ROPE_DIM // 2
    inv = ROPE_THETA ** (-np.arange(half, dtype=np.float32) / np.float32(half))
    ang = np.asarray(positions, np.float32)[:, None] * inv[None, :]
    c, s = np.cos(ang).astype(np.float32), np.sin(ang).astype(np.float32)
    n = ang.shape[0]

    def place(first):
        cos = np.ones((n, LANE), np.float32)
        sin = np.zeros((n, LANE), np.float32)
        cos[:, first:first + ROPE_DIM] = np.concatenate([c, c], axis=-1)
        sin[:, first:first + ROPE_DIM] = np.concatenate([-s, s], axis=-1)
        return jnp.asarray(cos), jnp.asarray(sin)

    return place(NOPE_DIM) + place(0)


def _silu(x):
    return x * (1.0 / (1.0 + jnp.exp(-x)))


def _ffn_prompt_kernel(x_ref, g_ref, wg_ref, wv_ref, cw_ref, wd_ref, gf_ref,
                       y_ref, st_ref, carry_ref, acc_ref, *, nchunk, final_norm):
    t = pl.program_id(1)
    nt = pl.num_programs(1)

    @pl.when(t == 0)
    def _():
        carry_ref[...] = jnp.zeros_like(carry_ref)

    x = x_ref[0]
    tm = x.shape[0]
    hb = _rms(x, g_ref[...]).astype(BF16)
    acc_ref[...] = x
    row = lax.broadcasted_iota(I32, (tm, FFN_CK), 0)

    def conv_part(c, part, w_ref):
        u = jnp.dot(hb, w_ref[c], preferred_element_type=F32)
        idx = part * nchunk + c
        prev = carry_ref[idx]
        p1, p2 = prev[7:8], prev[6:7]
        u1 = jnp.where(row == 0, p1, pltpu.roll(u, 1, 0))
        u2 = jnp.where(row == 0, p2, jnp.where(row == 1, p1, pltpu.roll(u, 2, 0)))
        cw = cw_ref[idx]
        carry_ref[idx, 6:8, :] = u[tm - 2:tm]
        return cw[3:4] + cw[2:3] * u + cw[1:2] * u1 + cw[0:1] * u2

    def body(c, _):
        gate = conv_part(c, 0, wg_ref)
        val = conv_part(c, 1, wv_ref)
        act = (_silu(gate) * val).astype(BF16)
        acc_ref[...] += jnp.dot(act, wd_ref[c], preferred_element_type=F32)
        return 0

    lax.fori_loop(0, nchunk, body, 0)
    y = acc_ref[...]
    if final_norm:
        y = _rms(y, gf_ref[...])
    y_ref[0] = y

    @pl.when(t == nt - 1)
    def _():
        st_ref[0] = carry_ref[...]


def _ffn_weights(w_up, conv_w, conv_b, w_down):
    d, two_f = w_up.shape
    f = two_f // 2
    nc = f // FFN_CK
    wu = w_up.astype(BF16)
    wg = wu[:, :f].reshape(d, nc, FFN_CK).transpose(1, 0, 2)
    wv = wu[:, f:].reshape(d, nc, FFN_CK).transpose(1, 0, 2)
    wd = w_down.astype(BF16).reshape(nc, FFN_CK, d)
    cw = jnp.concatenate([conv_w, conv_b[None], jnp.zeros((8 - CONV_W - 1, two_f), F32)], axis=0)
    cw = cw.reshape(8, 2 * nc, FFN_CK).transpose(1, 0, 2)
    return wg, wv, wd, cw, nc


def _ffn_prompt(x, g, fw, g_final, final_norm, tm):
    b, s, d = x.shape
    wg, wv, wd, cw, nc = fw
    y, st = pl.pallas_call(
        functools.partial(_ffn_prompt_kernel, nchunk=nc, final_norm=final_norm),
        grid=(b, s // tm),
        in_specs=[pl.BlockSpec((1, tm, d), lambda i, t: (i, t, 0)), pl.BlockSpec((1, d), lambda i, t: (0, 0)),
                  pl.BlockSpec(wg.shape, lambda i, t: (0, 0, 0)), pl.BlockSpec(wv.shape, lambda i, t: (0, 0, 0)),
                  pl.BlockSpec(cw.shape, lambda i, t: (0, 0, 0)),
                  pl.BlockSpec(wd.shape, lambda i, t: (0, 0, 0)), pl.BlockSpec((1, d), lambda i, t: (0, 0))],
        out_specs=[pl.BlockSpec((1, tm, d), lambda i, t: (i, t, 0)),
                   pl.BlockSpec((1, 2 * nc, 8, FFN_CK), lambda i, t: (i, 0, 0, 0))],
        out_shape=[jax.ShapeDtypeStruct((b, s, d), F32), jax.ShapeDtypeStruct((b, 2 * nc, 8, FFN_CK), F32)],
        scratch_shapes=[pltpu.VMEM((2 * nc, 8, FFN_CK), F32), pltpu.VMEM((tm, d), F32)],
        compiler_params=_cparams(("parallel", "arbitrary")),
        name="ffn_prompt",
    )(x, g.reshape(1, d), wg, wv, cw, wd, g_final.reshape(1, d))
    st = st[:, :, 6:8, :].transpose(0, 2, 1, 3).reshape(b, 2, 2 * nc * FFN_CK)
    return y, st


def _ffn_sample_kernel(x_ref, g_ref, wg_ref, wv_ref, cw_ref, wd_ref, gf_ref, p0_ref, p1_ref,
                       y_ref, u_ref, acc_ref, *, nchunk, final_norm):
    x = x_ref[...]
    hb = _rms(x, g_ref[...]).astype(BF16)
    acc_ref[...] = x

    def conv_part(c, part, w_ref):
        u = jnp.dot(hb, w_ref[c], preferred_element_type=F32)
        idx = part * nchunk + c
        u_ref[idx] = u
        cw = cw_ref[idx]
        return cw[3:4] + cw[2:3] * u + cw[1:2] * p1_ref[idx] + cw[0:1] * p0_ref[idx]

    def body(c, _):
        gate = conv_part(c, 0, wg_ref)
        val = conv_part(c, 1, wv_ref)
        act = (_silu(gate) * val).astype(BF16)
        acc_ref[...] += jnp.dot(act, wd_ref[c], preferred_element_type=F32)
        return 0

    lax.fori_loop(0, nchunk, body, 0)
    y = acc_ref[...]
    if final_norm:
        y = _rms(y, gf_ref[...])
    y_ref[...] = y


def _ffn_sample(x, g, fw, g_final, final_norm, state):
    m, d = x.shape
    wg, wv, wd, cw, nc = fw
    st = state.reshape(m, 2, 2 * nc, FFN_CK).transpose(1, 2, 0, 3)
    full3 = lambda a: pl.BlockSpec(a.shape, lambda i: (0, 0, 0))
    full2 = lambda a: pl.BlockSpec(a.shape, lambda i: (0, 0))
    p0, p1 = st[0], st[1]
    y, u = pl.pallas_call(
        functools.partial(_ffn_sample_kernel, nchunk=nc, final_norm=final_norm),
        grid=(1,),
        in_specs=[full2(x), pl.BlockSpec((1, d), lambda i: (0, 0)), full3(wg), full3(wv), full3(cw), full3(wd),
                  pl.BlockSpec((1, d), lambda i: (0, 0)), full3(p0), full3(p1)],
        out_specs=[full2(x), full3(p0)],
        out_shape=[jax.ShapeDtypeStruct((m, d), F32), jax.ShapeDtypeStruct(p0.shape, F32)],
        scratch_shapes=[pltpu.VMEM((m, d), F32)],
        compiler_params=_cparams(("arbitrary",)),
        name="ffn_sample",
    )(x, g.reshape(1, d), wg, wv, cw, wd, g_final.reshape(1, d), p0, p1)
    u = u.transpose(1, 0, 2).reshape(m, 2 * nc * FFN_CK)
    return y, jnp.stack([state[:, 1], u], axis=1)


def _sortable(s):
    b = lax.bitcast_convert_type(s + 0.0, I32)
    return jnp.where(b < 0, b ^ 0x7FFFFFFF, b)


_KEY_NEG_INF = int(np.int32(np.array(-np.inf, np.float32).view(np.int32)) ^ np.int32(0x7FFFFFFF))


def _lane_total(part):
    return jnp.dot(part.astype(BF16), jnp.ones((LANE, LANE), BF16), preferred_element_type=F32)


def _count(key_ref, nchunks, rows, pred):
    kc = key_ref.shape[2]

    def body(c, acc):
        hit = jnp.where(pred(key_ref[c]), 1.0, 0.0)
        for t in range(kc // LANE):
            acc = acc + hit[:, t * LANE:(t + 1) * LANE]
        return acc

    return _lane_total(lax.fori_loop(0, nchunks, body, jnp.zeros((rows, LANE), F32)))


def _kth_largest(key_ref, nchunks, rows, k):
    kc = key_ref.shape[2]

    def bit_step(i, carry):
        thr, cnt_thr = carry
        cand = thr + lax.shift_left(jnp.int32(1), 31 - i)
        cand_w = _rep(cand, kc)
        cnt = _count(key_ref, nchunks, rows, lambda key: key >= cand_w)
        ok = cnt >= k
        return jnp.where(ok, cand, thr), jnp.where(ok, cnt, cnt_thr)

    total = (nchunks * kc).astype(F32) if not isinstance(nchunks, int) else float(nchunks * kc)
    init = (jnp.full((rows, LANE), INT_MIN, I32), jnp.zeros((rows, LANE), F32) + total)
    return lax.fori_loop(0, 32, bit_step, init)


def _write_select_mask(key_ref, ma_ref, nchunks, rows, k, thr, cnt_thr, tri_ref):
    kc = key_ref.shape[2]
    thr_w = _rep(thr, kc)
    need_ties = jnp.max(jnp.where((cnt_thr > k) & (thr > _KEY_NEG_INF), 1.0, 0.0)) > 0.0

    @pl.when(jnp.logical_not(need_ties))
    def _():
        def body(c, _):
            ma_ref[c] = jnp.where(key_ref[c] >= thr_w, 0.0, MASKED)
            return 0
        lax.fori_loop(0, nchunks, body, 0)

    @pl.when(need_ties)
    def _():
        n_gt = _count(key_ref, nchunks, rows, lambda key: key > thr_w)
        room_w = _rep(k - n_gt, kc)
        ones = jnp.ones((kc, LANE), BF16)

        def body(c, seen):
            key = key_ref[c]
            eq = key == thr_w
            eqb = jnp.where(eq, 1.0, 0.0).astype(BF16)
            prefix = _rep(seen, kc) + jnp.dot(eqb, tri_ref[...], preferred_element_type=F32)
            keep = (key > thr_w) | (eq & (prefix <= room_w))
            ma_ref[c] = jnp.where(keep, 0.0, MASKED)
            return seen + jnp.dot(eqb, ones, preferred_element_type=F32)
        lax.fori_loop(0, nchunks, body, jnp.zeros((rows, LANE), F32))


def _tri(n):
    return jnp.asarray(np.triu(np.ones((n, n), np.float32)), BF16)


def _ones_rows(n_ones, shape):
    return jnp.where(lax.broadcasted_iota(I32, shape, 0) < n_ones, 1.0, 0.0).astype(BF16)


def _ones_cols(n_ones, shape):
    return jnp.where(lax.broadcasted_iota(I32, shape, 1) < n_ones, 1.0, 0.0).astype(BF16)


def _causal_two_pass(qb, n_groups, rows, qa_ref, m_ref, acc_ref, logits_at, pv_at, tile_at, add_at, col_shift):
    n_stack = rows // TQ
    lane = lax.broadcasted_iota(I32, (rows, LANE), 1)

    def logits(g, c, off, w):
        s = logits_at(g, off, qa_ref[g])
        if add_at is not None:
            s = (s.reshape(n_stack, TQ, KC) + add_at(c)[None]).reshape(rows, KC)
        tile = tile_at(g, w) if w is not None else None
        return s if tile is None else s + tile

    def sweep(step):
        def far(c, _):
            step(c, None)
            return 0
        lax.fori_loop(0, jnp.maximum(qb - 1, 0), far, 0)

        @pl.when(qb >= 1)
        def _():
            step(qb - 1, 0)
        step(qb, 1)

    m_ref[...] = jnp.full(m_ref.shape, -jnp.inf, F32)

    def step_max(c, w):
        off = pl.multiple_of(c * KC, KC)
        for g in range(n_groups):
            s = logits(g, c, off, w)
            mx = s[:, :LANE]
            for t in range(1, KC // LANE):
                mx = jnp.maximum(mx, s[:, t * LANE:(t + 1) * LANE])
            m_ref[g] = jnp.maximum(m_ref[g], mx)

    sweep(step_max)
    for g in range(n_groups):
        m = jnp.max(m_ref[g], axis=-1, keepdims=True)
        qa_ref[g] = jnp.where(lane == col_shift, jnp.broadcast_to(-m, (rows, LANE)).astype(BF16), qa_ref[g])
    acc_ref[...] = jnp.zeros(acc_ref.shape, F32)

    def step_acc(c, w):
        off = pl.multiple_of(c * KC, KC)
        for g in range(n_groups):
            p = jnp.exp(logits(g, c, off, w)).astype(BF16)
            acc_ref[g] += pv_at(g, off, p)

    sweep(step_acc)


def _far_cols(far_ref, first, n, lane):
    hi = jnp.concatenate([jnp.full((TQ, LANE), far_ref[2 * (first + i)], F32) for i in range(n)], axis=0)
    lo = jnp.concatenate([jnp.full((TQ, LANE), far_ref[2 * (first + i) + 1], F32) for i in range(n)], axis=0)
    return jnp.where(lane == COL_BIAS_HI, hi, jnp.where(lane == COL_BIAS_LO, lo, 0.0))


_NT = (((1,), (1,)), ((), ()))


def _a_prompt_kernel(far_ref, qi_ref, kw_ref, kit_ref, q_ref, kt_ref, vt_ref, tile_ref, tri_ref,
                     o_ref, key_ref, ma_ref, wb_ref, qa_ref, m_ref, acc_ref, *, topk):
    qb = pl.program_id(1)
    rows = A_GROUP * TQ
    kw = kw_ref[0]
    for h in range(IDX_HEADS):
        wb_ref[h] = jnp.broadcast_to(kw[:, IDX_DIM + h:IDX_DIM + h + 1], (TQ, LANE)) * (IDX_HEADS ** -0.5)
    qi = qi_ref[0, 0]
    zpad = jnp.zeros((LANE - IDX_DIM, KC), BF16)

    def score(c):
        off = pl.multiple_of(c * KC, KC)
        kit = jnp.concatenate([kit_ref[0, :, pl.ds(off, KC)].astype(BF16), zpad], axis=0)
        d = jnp.maximum(jnp.dot(qi, kit, preferred_element_type=F32), 0.0)
        s = d[:TQ] * _rep(wb_ref[0], KC)
        for h in range(1, IDX_HEADS):
            s = s + d[h * TQ:(h + 1) * TQ] * _rep(wb_ref[h], KC)
        return s

    def score_body(c, _):
        key_ref[c] = _sortable(score(c))
        return 0

    lax.fori_loop(0, qb, score_body, 0)
    causal = (lax.broadcasted_iota(I32, (TQ, KC), 1) <= lax.broadcasted_iota(I32, (TQ, KC), 0))
    key_ref[qb] = _sortable(jnp.where(causal, score(qb), -jnp.inf))
    thr, cnt_thr = _kth_largest(key_ref, qb + 1, TQ, topk)
    _write_select_mask(key_ref, ma_ref, qb + 1, TQ, topk, thr, cnt_thr, tri_ref)
    lane = lax.broadcasted_iota(I32, (rows, LANE), 1)
    for kv in range(A_KV_HEADS):
        q = q_ref[0, 0, kv * A_GROUP:(kv + 1) * A_GROUP].reshape(rows, LANE)
        qa_ref[kv] = jnp.where(lane < HEAD_DIM, q, _far_cols(far_ref, kv * A_GROUP, A_GROUP, lane).astype(BF16))
    k_ones = _ones_rows(3, (LANE - HEAD_DIM, KC))
    v_ones = _ones_rows(1, (LANE - HEAD_DIM, KC))

    def logits_at(kv, off, qa):
        kt = jnp.concatenate([kt_ref[0, kv * HEAD_DIM:(kv + 1) * HEAD_DIM, pl.ds(off, KC)].astype(BF16), k_ones], axis=0)
        return jnp.dot(qa, kt, preferred_element_type=F32)

    def pv_at(kv, off, p):
        vt = jnp.concatenate([vt_ref[0, kv * HEAD_DIM:(kv + 1) * HEAD_DIM, pl.ds(off, KC)].astype(BF16), v_ones], axis=0)
        return lax.dot_general(p, vt, _NT, preferred_element_type=F32)

    _causal_two_pass(qb, A_KV_HEADS, rows, qa_ref, m_ref, acc_ref, logits_at, pv_at,
                     lambda kv, w: tile_ref[kv, w], lambda c: ma_ref[c], COL_SHIFT)
    for kv in range(A_KV_HEADS):
        acc = acc_ref[kv]
        o = acc[:, :HEAD_DIM] / acc[:, HEAD_DIM:HEAD_DIM + 1]
        for g in range(A_GROUP):
            h = kv * A_GROUP + g
            o_ref[0, :, h * HEAD_DIM:(h + 1) * HEAD_DIM] = o[g * TQ:(g + 1) * TQ].astype(o_ref.dtype)


def _a_prompt_attn(far, qi, kw, kit, qh, kt, vt, tiles, topk):
    b, nq = qh.shape[0], qh.shape[1]
    s = nq * TQ
    nc = s // KC
    rows = A_GROUP * TQ
    kvd = A_KV_HEADS * HEAD_DIM
    grid_spec = pltpu.PrefetchScalarGridSpec(
        num_scalar_prefetch=1,
        grid=(b, nq),
        in_specs=[
            pl.BlockSpec((1, 1, IDX_HEADS * TQ, LANE), lambda i, j, f: (i, j, 0, 0)),
            pl.BlockSpec((1, TQ, LANE), lambda i, j, f: (i, j, 0)),
            pl.BlockSpec((1, IDX_DIM, s), lambda i, j, f: (i, 0, 0)),
            pl.BlockSpec((1, 1, A_HEADS, TQ, LANE), lambda i, j, f: (i, j, 0, 0, 0)),
            pl.BlockSpec((1, kvd, s), lambda i, j, f: (i, 0, 0)),
            pl.BlockSpec((1, kvd, s), lambda i, j, f: (i, 0, 0)),
            pl.BlockSpec(tiles.shape, lambda i, j, f: (0, 0, 0, 0)),
            pl.BlockSpec((KC, KC), lambda i, j, f: (0, 0)),
        ],
        out_specs=pl.BlockSpec((1, TQ, A_HEADS * HEAD_DIM), lambda i, j, f: (i, j, 0)),
        scratch_shapes=[pltpu.VMEM((nc, TQ, KC), I32), pltpu.VMEM((nc, TQ, KC), F32),
                        pltpu.VMEM((IDX_HEADS, TQ, LANE), F32), pltpu.VMEM((A_KV_HEADS, rows, LANE), BF16),
                        pltpu.VMEM((A_KV_HEADS, rows, LANE), F32), pltpu.VMEM((A_KV_HEADS, rows, LANE), F32)],
    )
    return pl.pallas_call(
        functools.partial(_a_prompt_kernel, topk=topk),
        grid_spec=grid_spec,
        out_shape=jax.ShapeDtypeStruct((b, s, A_HEADS * HEAD_DIM), BF16),
        compiler_params=_cparams(("parallel", "arbitrary")),
        name="a_prompt_attn",
    )(far, qi, kw, kit, qh, kt, vt, tiles, _tri(KC))


def _b_prompt_kernel(far_ref, q_ref, kt_ref, v_ref, tile_ref, o0_ref, o1_ref, qa_ref, m_ref, acc_ref):
    qb = pl.program_id(1)
    hd = 2 * HEAD_DIM
    rows = B_GROUP * TQ
    ngrp = B_KV_HEADS * 2
    lane = lax.broadcasted_iota(I32, (rows, LANE), 1)
    for grp in range(ngrp):
        qa_ref[grp] = jnp.where(lane < HEAD_DIM, q_ref[0, 0, grp],
                                _far_cols(far_ref, grp * B_GROUP, B_GROUP, lane).astype(BF16))
    k_ones = _ones_rows(3, (LANE - HEAD_DIM, KC))
    v_ones = _ones_cols(1, (KC, LANE))

    def logits_at(grp, off, qa):
        kt = jnp.concatenate([kt_ref[0, grp * HEAD_DIM:(grp + 1) * HEAD_DIM, pl.ds(off, KC)].astype(BF16), k_ones], axis=0)
        return jnp.dot(qa, kt, preferred_element_type=F32)

    def pv_at(grp, off, p):
        kv = grp // 2
        v = jnp.concatenate([v_ref[0, pl.ds(off, KC), kv * hd:(kv + 1) * hd].astype(BF16), v_ones], axis=1)
        return jnp.dot(p, v, preferred_element_type=F32)

    _causal_two_pass(qb, ngrp, rows, qa_ref, m_ref, acc_ref, logits_at, pv_at,
                     lambda grp, w: tile_ref[grp, w], None, COL_SHIFT)
    for grp in range(ngrp):
        kv, j = grp // 2, grp % 2
        acc = acc_ref[grp]
        o = acc[:, :hd] / acc[:, hd:hd + 1]
        o_ref = o0_ref if j == 0 else o1_ref
        for g in range(B_GROUP):
            h = kv * B_GROUP + g
            o_ref[0, :, h * hd:(h + 1) * hd] = o[g * TQ:(g + 1) * TQ]


def _b_prompt_attn(far, qh, kt, v, tiles):
    b, nq = qh.shape[0], qh.shape[1]
    s = nq * TQ
    hd = 2 * HEAD_DIM
    ngrp = B_KV_HEADS * 2
    rows = B_GROUP * TQ
    grid_spec = pltpu.PrefetchScalarGridSpec(
        num_scalar_prefetch=1,
        grid=(b, nq),
        in_specs=[
            pl.BlockSpec((1, 1, ngrp, rows, LANE), lambda i, j, f: (i, j, 0, 0, 0)),
            pl.BlockSpec((1, ngrp * HEAD_DIM, s), lambda i, j, f: (i, 0, 0)),
            pl.BlockSpec((1, s, B_KV_HEADS * hd), lambda i, j, f: (i, 0, 0)),
            pl.BlockSpec(tiles.shape, lambda i, j, f: (0, 0, 0, 0)),
        ],
        out_specs=[pl.BlockSpec((1, TQ, B_HEADS * hd), lambda i, j, f: (i, j, 0))] * 2,
        scratch_shapes=[pltpu.VMEM((ngrp, rows, LANE), BF16), pltpu.VMEM((ngrp, rows, LANE), F32),
                        pltpu.VMEM((ngrp, rows, 2 * LANE), F32)],
    )
    return pl.pallas_call(
        _b_prompt_kernel,
        grid_spec=grid_spec,
        out_shape=[jax.ShapeDtypeStruct((b, s, B_HEADS * hd), F32)] * 2,
        compiler_params=_cparams(("parallel", "arbitrary")),
        name="b_prompt_attn",
    )(far, qh, kt, v, tiles)


def _c_prompt_kernel(q_ref, k_ref, v_ref, tile_ref, o_ref, qa_ref, m_ref, acc_ref):
    qb = pl.program_id(1)
    for h in range(C_HEADS):
        qa_ref[h] = q_ref[0, :, h * LANE:(h + 1) * LANE]

    def logits_at(h, off, qa):
        return lax.dot_general(qa, k_ref[0, pl.ds(off, KC), h * LANE:(h + 1) * LANE], _NT, preferred_element_type=F32)

    def pv_at(h, off, p):
        return jnp.dot(p, v_ref[0, pl.ds(off, KC), h * LANE:(h + 1) * LANE], preferred_element_type=F32)

    _causal_two_pass(qb, C_HEADS, TQ, qa_ref, m_ref, acc_ref, logits_at, pv_at,
                     lambda h, w: tile_ref[...] if w == 1 else None, None, C_COL_SHIFT)
    for h in range(C_HEADS):
        acc = acc_ref[h]
        o_ref[0, :, h * V_DIM:(h + 1) * V_DIM] = (acc[:, :V_DIM] / acc[:, V_DIM:V_DIM + 1]).astype(o_ref.dtype)


def _c_prompt_attn(q, k, v, tile):
    b, s, hw = q.shape
    nq = s // TQ
    return pl.pallas_call(
        _c_prompt_kernel,
        grid=(b, nq),
        in_specs=[pl.BlockSpec((1, TQ, hw), lambda i, j: (i, j, 0)),
                  pl.BlockSpec((1, s, hw), lambda i, j: (i, 0, 0)),
                  pl.BlockSpec((1, s, hw), lambda i, j: (i, 0, 0)),
                  pl.BlockSpec((TQ, KC), lambda i, j: (0, 0))],
        out_specs=pl.BlockSpec((1, TQ, C_HEADS * V_DIM), lambda i, j: (i, j, 0)),
        out_shape=jax.ShapeDtypeStruct((b, s, C_HEADS * V_DIM), BF16),
        scratch_shapes=[pltpu.VMEM((C_HEADS, TQ, LANE), BF16), pltpu.VMEM((C_HEADS, TQ, LANE), F32),
                        pltpu.VMEM((C_HEADS, TQ, LANE), F32)],
        compiler_params=_cparams(("parallel", "arbitrary")),
        name="c_prompt_attn",
    )(q, k, v, tile)


def _bias_tiles(rel_table, cols):
    i = np.arange(TQ)[:, None]
    j = np.arange(KC)[None, :]
    dist = np.stack([i - j + KC, i - j])
    assert np.all(_rel_bucket_np(np.arange(KC + 1, 16 * KC)) == REL_BUCKETS - 1)
    tab = rel_table.astype(F32)[:, np.asarray(cols)]
    far = tab[REL_BUCKETS - 1]
    hi = far.astype(BF16).astype(F32)
    lo = (far - hi).astype(BF16).astype(F32)
    tiles = jnp.transpose(_bias_lookup(tab, dist), (3, 0, 1, 2)) - (hi + lo)[:, None, None, None]
    tiles = jnp.where(jnp.asarray((dist >= 0)[None]), tiles, MASKED)
    return tiles, jnp.stack([hi, lo], axis=1).reshape(-1)


def _page_specs(block_tail, slot, n_pages, pps):
    specs = []
    for p in range(pps):
        def imap(i, g, pt, p=p):
            return (slot, pt[i * n_pages + g * pps + p]) + (0,) * len(block_tail)
        specs.append(pl.BlockSpec((None, None) + block_tail, imap))
    return specs


def _a_idx_kernel(pt_ref, qi_ref, wi_ref, *refs, pps):
    page_refs, o_ref = refs[:pps], refs[pps]
    qi = qi_ref[0]
    wi = wi_ref[0] * (IDX_HEADS ** -0.5)
    outs = []
    for p in range(pps):
        d = jnp.dot(qi, page_refs[p][...].astype(BF16), preferred_element_type=F32)
        outs.append(jnp.sum(jnp.maximum(d, 0.0) * wi, axis=0, keepdims=True))
    o_ref[0] = jnp.concatenate(outs, axis=-1)


def _a_idx_scores(page_table, qi, wi, pool_idx_t, slot):
    db = qi.shape[0]
    n_pages = page_table.shape[1]
    pps = IDX_PAGES_PER_STEP
    grid_spec = pltpu.PrefetchScalarGridSpec(
        num_scalar_prefetch=1,
        grid=(db, n_pages // pps),
        in_specs=[pl.BlockSpec((1, 2 * IDX_HEADS, IDX_DIM), lambda i, g, pt: (i, 0, 0)),
                  pl.BlockSpec((1, 2 * IDX_HEADS, 1), lambda i, g, pt: (i, 0, 0))]
        + _page_specs((IDX_DIM, PAGE_SIZE), slot, n_pages, pps),
        out_specs=pl.BlockSpec((1, 1, pps * PAGE_SIZE), lambda i, g, pt: (i, 0, g)),
    )
    return pl.pallas_call(
        functools.partial(_a_idx_kernel, pps=pps),
        grid_spec=grid_spec,
        out_shape=jax.ShapeDtypeStruct((db, 1, n_pages * PAGE_SIZE), F32),
        compiler_params=_cparams(("parallel", "arbitrary")),
        name="a_idx_scores",
    )(page_table.reshape(-1), qi, wi, *([pool_idx_t] * pps))


def _a_select_kernel(sp_ref, qi_ref, ki_ref, wi_ref, tri_ref, ma_ref, key_ref, *, topk, n_past):
    rows = sp_ref.shape[0]
    nch = n_past // KC
    for c in range(nch):
        key_ref[c] = _sortable(sp_ref[:, c * KC:(c + 1) * KC])
    d = jnp.sum(qi_ref[...] * ki_ref[...], axis=-1)
    s_new = jnp.sum(jnp.maximum(d, 0.0) * (wi_ref[...] * (IDX_HEADS ** -0.5)), axis=-1, keepdims=True)
    lane = lax.broadcasted_iota(I32, (rows, KC), 1)
    key_ref[nch] = _sortable(jnp.where(lane == 0, s_new, -jnp.inf))
    thr, cnt_thr = _kth_largest(key_ref, nch + 1, rows, topk)
    _write_select_mask(key_ref, ma_ref, nch + 1, rows, topk, thr, cnt_thr, tri_ref)


def _a_select(scores_past, qi, ki, wi, topk):
    db, n_past = scores_past.shape
    nch = n_past // KC
    args = (scores_past, qi, ki, wi, _tri(KC))
    return pl.pallas_call(
        functools.partial(_a_select_kernel, topk=topk, n_past=n_past),
        grid=(1,),
        in_specs=[pl.BlockSpec(a.shape, lambda i, nd=a.ndim: (0,) * nd) for a in args],
        out_specs=pl.BlockSpec((nch + 1, db, KC), lambda i: (0, 0, 0)),
        out_shape=jax.ShapeDtypeStruct((nch + 1, db, KC), F32),
        scratch_shapes=[pltpu.VMEM((nch + 1, db, KC), I32)],
        compiler_params=_cparams(("arbitrary",)),
        name="a_select",
    )(*args)


def _paged_attn_kernel(pt_ref, *refs, pps, k_feature_major, v_mode, n_interleave, has_mask, has_bias):
    n_kparts = len(k_feature_major)
    g = pl.program_id(1)
    ng = pl.num_programs(1)
    it = iter(refs)
    q_refs = [next(it) for _ in range(n_kparts)]
    kn_refs = [next(it) for _ in range(n_kparts)]
    vn_ref = next(it)
    an_ref = next(it)
    bias_ref = next(it) if has_bias else None
    mask_ref = next(it) if has_mask else None
    k_pages = [[next(it) for _ in range(pps)] for _ in range(n_kparts)]
    v_pages = k_pages[0] if v_mode == 'share0' else [next(it) for _ in range(pps)]
    o_ref, m_ref, l_ref, acc_ref = next(it), next(it), next(it), next(it)
    rows = acc_ref.shape[0]

    @pl.when(g == 0)
    def _():
        s_new = an_ref[0]
        for q_ref, kn_ref in zip(q_refs, kn_refs):
            s_new = s_new + jnp.sum(q_ref[0].astype(F32) * kn_ref[0], axis=-1, keepdims=True)
        m_ref[...] = s_new
        l_ref[...] = jnp.ones_like(l_ref)
        acc_ref[...] = jnp.broadcast_to(vn_ref[0], acc_ref.shape)

    parts = []
    shared = []
    for p in range(pps):
        s = None
        for part in range(n_kparts):
            kb = k_pages[part][p][...].astype(BF16)
            if part == 0 and v_mode == 'share0':
                shared.append(kb)
            if k_feature_major[part]:
                d = jnp.dot(q_refs[part][0], kb, preferred_element_type=F32)
            else:
                d = lax.dot_general(q_refs[part][0], kb, _NT, preferred_element_type=F32)
            s = d if s is None else s + d
        parts.append(s)
    s = jnp.concatenate(parts, axis=-1)
    if has_bias:
        s = s + bias_ref[...]
    if has_mask:
        s = s + mask_ref[0]
    m = m_ref[...]
    m_new = jnp.maximum(m, jnp.max(s, axis=-1, keepdims=True))
    alpha = jnp.exp(m - m_new)
    p_all = jnp.exp(s - m_new)
    l_ref[...] = alpha * l_ref[...] + jnp.sum(p_all, axis=-1, keepdims=True)
    pb = p_all.astype(BF16)
    acc = alpha * acc_ref[...]
    if v_mode == 'interleaved':
        row_kv = lax.broadcasted_iota(I32, acc.shape, 0) // (rows // n_interleave)
    for p in range(pps):
        pp = pb[:, p * PAGE_SIZE:(p + 1) * PAGE_SIZE]
        if v_mode == 'share0':
            acc = acc + jnp.dot(pp, shared[p], preferred_element_type=F32)
        elif v_mode == 'feature_major':
            acc = acc + lax.dot_general(pp, v_pages[p][...].astype(BF16), _NT, preferred_element_type=F32)
        else:
            upd = jnp.zeros(acc.shape, F32)
            for kv in range(n_interleave):
                v_kv = v_pages[p][pl.ds(kv, PAGE_SIZE, stride=n_interleave), :].astype(BF16)
                upd = jnp.where(row_kv == kv, jnp.dot(pp, v_kv, preferred_element_type=F32), upd)
            acc = acc + upd
    acc_ref[...] = acc
    m_ref[...] = m_new

    @pl.when(g == ng - 1)
    def _():
        o_ref[0] = acc_ref[...] / l_ref[...]


def _paged_attn(page_table, slot, q_parts, knew_parts, v_new, add_new, bias, mask, k_pools, k_feature_major,
                v_pool, v_mode, n_interleave=1):
    db, rows = q_parts[0].shape[:2]
    n_pages = page_table.shape[1]
    page_bytes = sum(int(np.prod(p.shape[2:])) * 4 for p in list(k_pools) + ([] if v_pool is None else [v_pool]))
    pps = n_pages
    while pps > 1 and (2 * pps * page_bytes > PAGE_VMEM_BUDGET or pps > MAX_PAGES_PER_STEP):
        pps //= 2
    dv = v_new.shape[-1]
    in_specs, args = [], []
    for q in q_parts:
        in_specs.append(pl.BlockSpec((1, rows, q.shape[-1]), lambda i, g, pt: (i, 0, 0)))
        args.append(q)
    for kn in knew_parts + [v_new]:
        in_specs.append(pl.BlockSpec((1,) + kn.shape[1:], lambda i, g, pt: (i, 0, 0)))
        args.append(kn)
    in_specs.append(pl.BlockSpec((1, rows, 1), lambda i, g, pt: (i, 0, 0)))
    args.append(add_new)
    if bias is not None:
        in_specs.append(pl.BlockSpec((rows, pps * PAGE_SIZE), lambda i, g, pt: (0, g)))
        args.append(bias)
    if mask is not None:
        in_specs.append(pl.BlockSpec((1, 1, pps * PAGE_SIZE), lambda i, g, pt: (i, 0, g)))
        args.append(mask)
    for pool in k_pools:
        in_specs += _page_specs(tuple(pool.shape[2:]), slot, n_pages, pps)
        args += [pool] * pps
    if v_mode != 'share0':
        in_specs += _page_specs(tuple(v_pool.shape[2:]), slot, n_pages, pps)
        args += [v_pool] * pps
    grid_spec = pltpu.PrefetchScalarGridSpec(
        num_scalar_prefetch=1,
        grid=(db, n_pages // pps),
        in_specs=in_specs,
        out_specs=pl.BlockSpec((1, rows, dv), lambda i, g, pt: (i, 0, 0)),
        scratch_shapes=[pltpu.VMEM((rows, 1), F32), pltpu.VMEM((rows, 1), F32), pltpu.VMEM((rows, dv), F32)],
    )
    return pl.pallas_call(
        functools.partial(_paged_attn_kernel, pps=pps, k_feature_major=tuple(k_feature_major), v_mode=v_mode,
                          n_interleave=n_interleave, has_mask=mask is not None, has_bias=bias is not None),
        grid_spec=grid_spec,
        out_shape=jax.ShapeDtypeStruct((db, rows, dv), F32),
        compiler_params=_cparams(("parallel", "arbitrary")),
        name="paged_attn",
    )(page_table.reshape(-1), *args)


def _c_post_kernel(o_ref, wuv_ref, wo_ref, x_ref, y_ref):
    parts = []
    for h in range(C_HEADS):
        parts.append(jnp.dot(o_ref[h].astype(BF16), wuv_ref[h], preferred_element_type=F32))
    o = jnp.concatenate(parts, axis=-1).astype(BF16)
    y_ref[...] = x_ref[...] + jnp.dot(o, wo_ref[...], preferred_element_type=F32)


def _c_post(o_lat, w_uv, w_o, x):
    args = (o_lat, w_uv, w_o, x)
    return pl.pallas_call(
        _c_post_kernel,
        grid=(1,),
        in_specs=[pl.BlockSpec(a.shape, lambda i, nd=a.ndim: (0,) * nd) for a in args],
        out_specs=pl.BlockSpec(x.shape, lambda i: (0, 0)),
        out_shape=jax.ShapeDtypeStruct(x.shape, F32),
        compiler_params=_cparams(("arbitrary",)),
        name="c_post",
    )(*args)


def _q_lat_kernel(qn_ref, wuk_ref, o_ref):
    for h in range(C_HEADS):
        o_ref[h] = lax.dot_general(qn_ref[h], wuk_ref[h], _NT, preferred_element_type=F32).astype(o_ref.dtype)


def _q_lat(qn, w_uk_h):
    args = (qn, w_uk_h)
    h, m = qn.shape[:2]
    return pl.pallas_call(
        _q_lat_kernel,
        grid=(1,),
        in_specs=[pl.BlockSpec(a.shape, lambda i, nd=a.ndim: (0,) * nd) for a in args],
        out_specs=pl.BlockSpec((h, m, KV_LORA), lambda i: (0, 0, 0)),
        out_shape=jax.ShapeDtypeStruct((h, m, KV_LORA), BF16),
        compiler_params=_cparams(("arbitrary",)),
        name="c_q_lat",
    )(*args)


def _blockdiag_q(q, n_blocks):
    db, nb, g, d = q.shape
    eye = jnp.eye(nb, dtype=q.dtype)
    return jnp.einsum('bkgd,kc->bkgcd', q, eye).reshape(db, nb * g, nb * d)


def _diag_blocks(o, n_blocks, g):
    db, rows, cols = o.shape
    e = cols // n_blocks
    o = o.reshape(db, n_blocks, g, n_blocks, e)
    return jnp.stack([o[:, k, :, k] for k in range(n_blocks)], axis=1)


def _feature_major_pages(pool):
    nd = pool.ndim
    t = jnp.transpose(pool, (0, 1) + tuple(range(3, nd)) + (2,))
    return t.reshape(pool.shape[0], pool.shape[1], -1, pool.shape[2])


def _mixer_a(xp, xs, g, w_in, w_o, rel_table, pool_k, pool_v, pool_i, slot, page_table):
    b, s, d = xp.shape
    db = xs.shape[0]
    wq, wk, wv, wqi, wki, wwi = [w.astype(BF16) for w in jnp.split(w_in, np.cumsum(A_SIZES)[:-1].tolist(), axis=-1)]
    wkw = jnp.pad(jnp.concatenate([wki, wwi], axis=1), ((0, 0), (0, LANE - IDX_DIM - IDX_HEADS)))
    wo = w_o.astype(BF16)
    cols = list(range(A_HEADS))
    q_scale, qi_scale = HEAD_DIM ** -0.5, IDX_DIM ** -0.5
    qh, qi, kt, vt, kw, kit = _a_proj(xp, g, _pad_heads(wq * q_scale, A_HEADS, HEAD_DIM), wk.T, wv.T,
                                      _pad_heads(wqi * qi_scale, IDX_HEADS, IDX_DIM), wkw, wki.T)
    topk = min(TOPK_MAX, s // 4)
    tiles, far = _bias_tiles(rel_table, cols)
    tiles = tiles.reshape(A_KV_HEADS, A_GROUP, 2, TQ, KC).transpose(0, 2, 1, 3, 4).reshape(A_KV_HEADS, 2, A_GROUP * TQ, KC)
    attn = _a_prompt_attn(far, qi, kw, kit, qh, kt, vt, tiles, topk)
    yp = _out_proj(attn.reshape(b * s, -1), wo, xp.reshape(b * s, d), 512).reshape(b, s, d)
    rows_p = (kt.reshape(b, A_KV_HEADS, HEAD_DIM, s).transpose(0, 3, 1, 2),
              vt.reshape(b, A_KV_HEADS, HEAD_DIM, s).transpose(0, 3, 1, 2), kit.transpose(0, 2, 1))
    n_pages = page_table.shape[1]
    past = n_pages * PAGE_SIZE
    q, k, v, qi, kw = _norm_proj(xs.reshape(db, d), g, [wq, wk, wv, wqi, wkw], [BF16, F32, F32, BF16, F32],
                                 [q_scale, 1.0, 1.0, qi_scale, 1.0], db)
    ki, wi = kw[:, :IDX_DIM], kw[:, IDX_DIM:IDX_DIM + IDX_HEADS]
    qi3 = qi.reshape(db, IDX_HEADS, IDX_DIM)
    sc = _a_idx_scores(page_table, jnp.pad(qi3, ((0, 0), (0, IDX_HEADS), (0, 0))),
                       jnp.pad(wi.reshape(db, IDX_HEADS, 1), ((0, 0), (0, IDX_HEADS), (0, 0))),
                       _feature_major_pages(pool_i), slot)
    topk_s = min(TOPK_MAX, (past + 1) // 4)
    ma = _a_select(sc.reshape(db, past), qi3.astype(F32), ki.reshape(db, 1, IDX_DIM), wi, topk_s)
    mask = ma[:past // KC].transpose(1, 0, 2).reshape(db, 1, past)
    tab = rel_table.astype(F32)
    bias = _bias_lookup(tab, past - np.arange(past)).T
    add_new = tab[0][None, :, None] + ma[past // KC, :, 0][:, None, None]
    qm = _blockdiag_q(q.reshape(db, A_KV_HEADS, A_GROUP, HEAD_DIM), A_KV_HEADS)
    o = _paged_attn(page_table, slot, [qm], [k.reshape(db, 1, -1)], v.reshape(db, 1, -1), add_new, bias, mask,
                    [_feature_major_pages(pool_k)], [True], _feature_major_pages(pool_v), 'feature_major')
    o = _diag_blocks(o, A_KV_HEADS, A_GROUP).reshape(db, A_HEADS * HEAD_DIM)
    ys = _out_proj(o, wo, xs.reshape(db, d), db).reshape(db, 1, d)
    rows_s = (k.reshape(db, 1, A_KV_HEADS, HEAD_DIM), v.reshape(db, 1, A_KV_HEADS, HEAD_DIM), ki.reshape(db, 1, IDX_DIM))
    return yp, ys, rows_p, rows_s


def _mixer_b(xp, xs, g, w_in, lam_p, subln, w_o, rel_table, lam_init, pool_k, pool_v, slot, page_table):
    b, s, d = xp.shape
    db = xs.shape[0]
    hd = 2 * HEAD_DIM
    wq, wk, wv = [w.astype(BF16) for w in jnp.split(w_in, np.cumsum(B_SIZES)[:-1].tolist(), axis=-1)]
    wo = w_o.astype(BF16)
    ngrp = B_KV_HEADS * 2
    q_scale = HEAD_DIM ** -0.5
    cols = [2 * (kv * B_GROUP + gg) + j for kv in range(B_KV_HEADS) for j in range(2) for gg in range(B_GROUP)]
    wq_kjg = (wq * q_scale).reshape(d, B_KV_HEADS, B_GROUP, 2, HEAD_DIM).transpose(0, 1, 3, 2, 4).reshape(d, -1)
    qh, kt, v = _b_proj(xp, g, _pad_heads(wq_kjg, B_HEADS * 2, HEAD_DIM), wk.T, wv)
    tiles, far = _bias_tiles(rel_table, cols)
    tiles = tiles.reshape(ngrp, B_GROUP, 2, TQ, KC).transpose(0, 2, 1, 3, 4).reshape(ngrp, 2, B_GROUP * TQ, KC)
    o0, o1 = _b_prompt_attn(far, qh, kt, v, tiles)
    yp = _b_post(o0.reshape(b * s, -1), o1.reshape(b * s, -1), lam_p, subln, wo, xp.reshape(b * s, d),
                 lam_init, 512).reshape(b, s, d)
    rows_p = (kt.reshape(b, B_KV_HEADS, 2, HEAD_DIM, s).transpose(0, 4, 1, 2, 3), v.reshape(b, s, B_KV_HEADS, hd))
    n_pages = page_table.shape[1]
    past = n_pages * PAGE_SIZE
    q, k, v = _norm_proj(xs.reshape(db, d), g, [wq, wk, wv], [BF16, F32, F32], [q_scale, 1.0, 1.0], db)
    tab = rel_table.astype(F32)
    colr = np.asarray(cols).reshape(B_KV_HEADS, 2, B_GROUP).transpose(0, 2, 1).reshape(-1)
    bias = _bias_lookup(tab[:, colr], past - np.arange(past)).T
    add_new = jnp.broadcast_to(tab[0][colr][None, :, None], (db, len(cols), 1))
    q5 = q.reshape(db, B_KV_HEADS, B_GROUP, 2, HEAD_DIM)
    eye = jnp.eye(ngrp, dtype=q.dtype).reshape(B_KV_HEADS, 2, B_KV_HEADS, 2)
    qm = jnp.einsum('bkgjd,kjcm->bkgjcmd', q5, eye).reshape(db, ngrp * B_GROUP, ngrp * HEAD_DIM)
    v_new = jnp.repeat(v.reshape(db, B_KV_HEADS, hd), B_GROUP * 2, axis=1)
    pool_v_rows = pool_v.reshape(pool_v.shape[0], pool_v.shape[1], PAGE_SIZE * B_KV_HEADS, hd)
    o = _paged_attn(page_table, slot, [qm], [k.reshape(db, 1, -1)], v_new, add_new, bias, None,
                    [_feature_major_pages(pool_k)], [True], pool_v_rows, 'interleaved', B_KV_HEADS)
    o = o.reshape(db, B_KV_HEADS, B_GROUP, 2, hd)
    o0 = o[:, :, :, 0].reshape(db, B_HEADS * hd)
    o1 = o[:, :, :, 1].reshape(db, B_HEADS * hd)
    ys = _b_post(o0, o1, lam_p, subln, wo, xs.reshape(db, d), lam_init, db).reshape(db, 1, d)
    rows_s = (k.reshape(db, 1, B_KV_HEADS, 2, HEAD_DIM), v.reshape(db, 1, B_KV_HEADS, hd))
    return yp, ys, rows_p, rows_s


def _mixer_c(xp, xs, g, w_down, q_norm, w_uq, kv_norm, w_ukv, w_o, pool_ckv, pool_kpe, slot, page_table):
    b, s, d = xp.shape
    db = xs.shape[0]
    n_pages = page_table.shape[1]
    past = n_pages * PAGE_SIZE
    wd = jnp.pad(w_down.astype(BF16), ((0, 0), (0, Q_LORA + KV_LORA + LANE - w_down.shape[1])))
    wq_aug = _pad_heads(w_uq.astype(BF16), C_HEADS, C_QK)
    wkv = w_ukv.astype(BF16).reshape(KV_LORA, C_HEADS, NOPE_DIM + V_DIM)
    wuk_aug = _pad_heads(wkv[:, :, :NOPE_DIM].reshape(KV_LORA, -1), C_HEADS, NOPE_DIM)
    wuv_aug = _pad_heads(wkv[:, :, NOPE_DIM:].reshape(KV_LORA, -1), C_HEADS, V_DIM)
    wo = w_o.astype(BF16)
    q, k, v, ckv, kpet = _c_proj(xp, g, wd, q_norm, wq_aug, kv_norm, wuk_aug, wuv_aug, _rope_tables(np.arange(s)), TQ)
    i = np.arange(TQ)[:, None]
    j = np.arange(KC)[None, :]
    attn = _c_prompt_attn(q, k, v, jnp.asarray(np.where(i >= j, 0.0, MASKED).astype(np.float32)))
    yp = _out_proj(attn.reshape(b * s, -1), wo, xp.reshape(b * s, d), 512).reshape(b, s, d)
    rows_p = (ckv, kpet.transpose(0, 2, 1))
    tabs = tuple(jnp.broadcast_to(t, (db, LANE)) for t in _rope_tables(past + np.arange(1)))
    q, _, _, ckv, kpet = _c_proj(xs.reshape(1, db, d), g, wd, q_norm, wq_aug, kv_norm, wuk_aug, wuv_aug, tabs, db)
    q3 = q.reshape(db, C_HEADS, LANE)
    ckv, kpe = ckv.reshape(db, KV_LORA), kpet[0].T
    q_lat = _q_lat(q3[:, :, :NOPE_DIM].transpose(1, 0, 2), wkv[:, :, :NOPE_DIM].transpose(1, 0, 2)).transpose(1, 0, 2)
    o_lat = _paged_attn(page_table, slot, [q_lat, q3[:, :, NOPE_DIM:C_QK]],
                        [ckv.reshape(db, 1, KV_LORA), kpe.reshape(db, 1, ROPE_DIM)], ckv.reshape(db, 1, KV_LORA),
                        jnp.zeros((db, C_HEADS, 1), F32), None, None,
                        [pool_ckv, _feature_major_pages(pool_kpe)], [False, True], None, 'share0')
    ys = _c_post(o_lat.transpose(1, 0, 2), wkv[:, :, NOPE_DIM:].transpose(1, 0, 2), wo, xs.reshape(db, d)).reshape(db, 1, d)
    rows_s = (ckv.reshape(db, 1, KV_LORA), kpe.reshape(db, 1, ROPE_DIM))
    return yp, ys, rows_p, rows_s


def kernel(x_prompt, x_sample, cache_a_k, cache_a_v, cache_a_idx, cache_b_k, cache_b_v, cache_c_ckv, cache_c_kpe, state_conv, page_table, rel_table, norm_mix, norm_ffn, norm_final, a_w_in, a_w_o, b_w_in, b_lambda, b_subln, b_w_o, c_w_down, c_q_norm, c_w_uq, c_kv_norm, c_w_ukv, c_w_o, ffn_w_up, ffn_conv_w, ffn_conv_b, ffn_w_down):
    depth = norm_mix.shape[0]
    db = x_sample.shape[0]
    xp, xs = x_prompt, x_sample
    rows_p = {0: [], 1: [], 2: []}
    rows_s = {0: [], 1: [], 2: []}
    conv_p, conv_s = [], []
    for i in range(depth):
        kind, slot = i % N_MIXERS, i // N_MIXERS
        if kind == 0:
            yp, ys, rp, rs = _mixer_a(xp, xs, norm_mix[i], a_w_in[slot], a_w_o[slot], rel_table,
                                      cache_a_k, cache_a_v, cache_a_idx, slot, page_table)
        elif kind == 1:
            lam_init = 0.8 - 0.6 * math.exp(-0.3 * i)
            yp, ys, rp, rs = _mixer_b(xp, xs, norm_mix[i], b_w_in[slot], b_lambda[slot], b_subln[slot], b_w_o[slot],
                                      rel_table, lam_init, cache_b_k, cache_b_v, slot, page_table)
        else:
            yp, ys, rp, rs = _mixer_c(xp, xs, norm_mix[i], c_w_down[slot], c_q_norm[slot], c_w_uq[slot],
                                      c_kv_norm[slot], c_w_ukv[slot], c_w_o[slot],
                                      cache_c_ckv, cache_c_kpe, slot, page_table)
        rows_p[kind].append(rp)
        rows_s[kind].append(rs)
        xp, xs = yp, ys
        fw = _ffn_weights(ffn_w_up[i], ffn_conv_w[i], ffn_conv_b[i], ffn_w_down[i])
        last = i == depth - 1
        xp, cp = _ffn_prompt(xp, norm_ffn[i], fw, norm_final, last, 512)
        xs2, cs = _ffn_sample(xs.reshape(db, -1), norm_ffn[i], fw, norm_final, last, state_conv[i])
        xs = xs2.reshape(xs.shape)
        conv_p.append(cp)
        conv_s.append(cs)

    def stack(rows, j):
        return jnp.stack([r[j] for r in rows])

    a_p, b_p, c_p = rows_p[0], rows_p[1], rows_p[2]
    a_s, b_s, c_s = rows_s[0], rows_s[1], rows_s[2]
    return (xp, xs,
            stack(a_p, 0), stack(a_p, 1), stack(a_p, 2), stack(b_p, 0), stack(b_p, 1),
            stack(c_p, 0), stack(c_p, 1), jnp.stack(conv_p),
            stack(a_s, 0), stack(a_s, 1), stack(a_s, 2), stack(b_s, 0), stack(b_s, 1),
            stack(c_s, 0), stack(c_s, 1), jnp.stack(conv_s))
```

```python
import functools
import math

import numpy as np
import jax
import jax.numpy as jnp
from jax import lax
from jax.experimental import pallas as pl
from jax.experimental.pallas import tpu as pltpu

F32 = jnp.float32
BF16 = jnp.bfloat16
I32 = jnp.int32

N_MIXERS = 3
HEAD_DIM = 64
PAGE_SIZE = 128
A_HEADS, A_KV_HEADS = 16, 4
A_GROUP = A_HEADS // A_KV_HEADS
IDX_HEADS, IDX_DIM = 8, 64
TOPK_MAX = 256
A_SIZES = (A_HEADS * HEAD_DIM, A_KV_HEADS * HEAD_DIM, A_KV_HEADS * HEAD_DIM,
           IDX_HEADS * IDX_DIM, IDX_DIM, IDX_HEADS)
B_HEADS, B_KV_HEADS = 8, 4
B_GROUP = B_HEADS // B_KV_HEADS
B_SIZES = (B_HEADS * 2 * HEAD_DIM, B_KV_HEADS * 2 * HEAD_DIM, B_KV_HEADS * 2 * HEAD_DIM)
C_HEADS = 16
Q_LORA, KV_LORA, NOPE_DIM, ROPE_DIM, V_DIM = 384, 256, 64, 32, 64
C_QK = NOPE_DIM + ROPE_DIM
C_SCALE = C_QK ** -0.5
ROPE_THETA = 10000.0
REL_BUCKETS, REL_MAX_DIST, N_BIAS = 32, 128, 16
CONV_W = 3
EPS = 1e-6

LANE = 128
TQ = 256
KC = 256
FFN_CK = 256
MAX_PAGES_PER_STEP = 32
PAGE_VMEM_BUDGET = 20 * 1024 * 1024
IDX_PAGES_PER_STEP = 32
MASKED = -1e30
INT_MIN = -2 ** 31
VMEM_LIMIT = 56 * 1024 * 1024
COL_BIAS_HI, COL_BIAS_LO, COL_SHIFT = HEAD_DIM, HEAD_DIM + 1, HEAD_DIM + 2
C_COL_SHIFT = C_QK


def _cparams(sem):
    return pltpu.CompilerParams(dimension_semantics=sem, vmem_limit_bytes=VMEM_LIMIT)


def _rms(x, g):
    return x * lax.rsqrt(jnp.mean(x * x, axis=-1, keepdims=True) + EPS) * g


def _rel_bucket_np(dist):
    n = np.maximum(np.asarray(dist, np.int64), 0)
    max_exact = REL_BUCKETS // 2
    nf = np.maximum(n, 1).astype(np.float32)
    large = max_exact + (np.log(nf / np.float32(max_exact)) / np.float32(math.log(REL_MAX_DIST / max_exact))
                         * np.float32(REL_BUCKETS - max_exact)).astype(np.int32)
    return np.where(n < max_exact, n, np.minimum(large, REL_BUCKETS - 1)).astype(np.int32)


def _bias_lookup(table, dist):
    bucket = _rel_bucket_np(dist)
    onehot = jnp.asarray(bucket.reshape(-1, 1) == np.arange(REL_BUCKETS)[None, :], F32)
    out = jnp.dot(onehot, table.astype(F32), precision=lax.Precision.HIGHEST)
    return out.reshape(bucket.shape + (table.shape[1],))


def _rep(x, n):
    return x if n == LANE else jnp.concatenate([x] * (n // LANE), axis=1)


def _norm_proj_kernel(x_ref, g_ref, *refs, n_out, scales):
    w_refs, o_refs = refs[:n_out], refs[n_out:]
    hb = _rms(x_ref[...], g_ref[...]).astype(BF16)
    for w_ref, o_ref, s in zip(w_refs, o_refs, scales):
        y = jnp.dot(hb, w_ref[...], preferred_element_type=F32)
        if s != 1.0:
            y = y * s
        o_ref[...] = y.astype(o_ref.dtype)


def _norm_proj(x, g, ws, dtypes, scales, tm):
    m, d = x.shape
    n_out = len(ws)
    return pl.pallas_call(
        functools.partial(_norm_proj_kernel, n_out=n_out, scales=tuple(scales)),
        grid=(m // tm,),
        in_specs=[pl.BlockSpec((tm, d), lambda i: (i, 0)), pl.BlockSpec((1, d), lambda i: (0, 0))]
        + [pl.BlockSpec(w.shape, lambda i: (0, 0)) for w in ws],
        out_specs=[pl.BlockSpec((tm, w.shape[1]), lambda i: (i, 0)) for w in ws],
        out_shape=[jax.ShapeDtypeStruct((m, w.shape[1]), dt) for w, dt in zip(ws, dtypes)],
        compiler_params=_cparams(("parallel",)),
        name="norm_proj",
    )(x, g.reshape(1, d), *ws)


def _out_proj_kernel(a_ref, w_ref, x_ref, o_ref):
    o_ref[...] = x_ref[...] + jnp.dot(a_ref[...].astype(BF16), w_ref[...], preferred_element_type=F32)


def _out_proj(a, w, x, tm):
    m, k = a.shape
    d = w.shape[1]
    return pl.pallas_call(
        _out_proj_kernel,
        grid=(m // tm,),
        in_specs=[pl.BlockSpec((tm, k), lambda i: (i, 0)), pl.BlockSpec((k, d), lambda i: (0, 0)),
                  pl.BlockSpec((tm, d), lambda i: (i, 0))],
        out_specs=pl.BlockSpec((tm, d), lambda i: (i, 0)),
        out_shape=jax.ShapeDtypeStruct((m, d), F32),
        compiler_params=_cparams(("parallel",)),
        name="out_proj",
    )(a, w, x)


def _b_post_kernel(o0_ref, o1_ref, lam_ref, sub_ref, w_ref, x_ref, o_ref, *, lam_init, n_heads):
    lp = lam_ref[...]
    lam = (jnp.exp(jnp.sum(lp[0:1] * lp[1:2], axis=-1, keepdims=True))
           - jnp.exp(jnp.sum(lp[2:3] * lp[3:4], axis=-1, keepdims=True)) + lam_init)
    d = o0_ref[...] - lam * o1_ref[...]
    hd = 2 * HEAD_DIM
    parts = []
    for h in range(n_heads):
        dh = d[:, h * hd:(h + 1) * hd]
        parts.append(_rms(dh, sub_ref[...]) * (1.0 - lam_init))
    dn = jnp.concatenate(parts, axis=-1).astype(BF16)
    o_ref[...] = x_ref[...] + jnp.dot(dn, w_ref[...], preferred_element_type=F32)


def _b_post(o0, o1, lam_p, subln, w_o, x, lam_init, tm):
    m, k = o0.shape
    d = w_o.shape[1]
    return pl.pallas_call(
        functools.partial(_b_post_kernel, lam_init=lam_init, n_heads=B_HEADS),
        grid=(m // tm,),
        in_specs=[pl.BlockSpec((tm, k), lambda i: (i, 0)), pl.BlockSpec((tm, k), lambda i: (i, 0)),
                  pl.BlockSpec(lam_p.shape, lambda i: (0, 0)), pl.BlockSpec((1, 2 * HEAD_DIM), lambda i: (0, 0)),
                  pl.BlockSpec((k, d), lambda i: (0, 0)), pl.BlockSpec((tm, d), lambda i: (i, 0))],
        out_specs=pl.BlockSpec((tm, d), lambda i: (i, 0)),
        out_shape=jax.ShapeDtypeStruct((m, d), F32),
        compiler_params=_cparams(("parallel",)),
        name="b_post",
    )(o0, o1, lam_p, subln.reshape(1, -1), w_o, x)


def _pad_heads(w, n_heads, dk):
    d = w.shape[0]
    return jnp.pad(w.reshape(d, n_heads, dk), ((0, 0), (0, 0), (0, LANE - dk))).reshape(d, n_heads * LANE)


def _a_proj_kernel(x_ref, g_ref, wq_ref, wk_ref, wv_ref, wqi_ref, wkw_ref, wkwt_ref,
                   qh_ref, qit_ref, kt_ref, vt_ref, kw_ref, kwt_ref):
    hb = _rms(x_ref[0], g_ref[...]).astype(BF16)
    yq = jnp.dot(hb, wq_ref[...], preferred_element_type=F32)
    for h in range(A_HEADS):
        qh_ref[0, 0, h] = yq[:, h * LANE:(h + 1) * LANE].astype(BF16)
    qit_ref[0] = lax.dot_general(wqi_ref[...], hb, _NT, preferred_element_type=F32).astype(BF16)
    kt_ref[0] = lax.dot_general(wk_ref[...], hb, _NT, preferred_element_type=F32)
    vt_ref[0] = lax.dot_general(wv_ref[...], hb, _NT, preferred_element_type=F32)
    kw_ref[0] = jnp.dot(hb, wkw_ref[...], preferred_element_type=F32)
    kwt_ref[0] = lax.dot_general(wkwt_ref[...], hb, _NT, preferred_element_type=F32)


def _a_proj(x, g, wq, wk_t, wv_t, wqi_t, wkw):
    b, s, d = x.shape
    nq = s // TQ
    kvd = A_KV_HEADS * HEAD_DIM
    qid = IDX_HEADS * IDX_DIM
    wkw_t = wkw.T
    full = lambda a: pl.BlockSpec(a.shape, lambda i, j: (0, 0))
    return pl.pallas_call(
        _a_proj_kernel,
        grid=(b, nq),
        in_specs=[pl.BlockSpec((1, TQ, d), lambda i, j: (i, j, 0)), pl.BlockSpec((1, d), lambda i, j: (0, 0)),
                  full(wq), full(wk_t), full(wv_t), full(wqi_t), full(wkw), full(wkw_t)],
        out_specs=[pl.BlockSpec((1, 1, A_HEADS, TQ, LANE), lambda i, j: (i, j, 0, 0, 0)),
                   pl.BlockSpec((1, qid, TQ), lambda i, j: (i, 0, j)),
                   pl.BlockSpec((1, kvd, TQ), lambda i, j: (i, 0, j)),
                   pl.BlockSpec((1, kvd, TQ), lambda i, j: (i, 0, j)),
                   pl.BlockSpec((1, TQ, LANE), lambda i, j: (i, j, 0)),
                   pl.BlockSpec((1, LANE, TQ), lambda i, j: (i, 0, j))],
        out_shape=[jax.ShapeDtypeStruct((b, nq, A_HEADS, TQ, LANE), BF16),
                   jax.ShapeDtypeStruct((b, qid, s), BF16),
                   jax.ShapeDtypeStruct((b, kvd, s), F32), jax.ShapeDtypeStruct((b, kvd, s), F32),
                   jax.ShapeDtypeStruct((b, s, LANE), F32), jax.ShapeDtypeStruct((b, LANE, s), F32)],
        compiler_params=_cparams(("parallel", "parallel")),
        name="a_proj",
    )(x, g.reshape(1, d), wq, wk_t, wv_t, wqi_t, wkw, wkw_t)


def _b_proj_kernel(x_ref, g_ref, wq_ref, wk_ref, wv_ref, qh_ref, kt_ref, v_ref):
    hb = _rms(x_ref[0], g_ref[...]).astype(BF16)
    yq = jnp.dot(hb, wq_ref[...], preferred_element_type=F32)
    for grp in range(B_KV_HEADS * 2):
        for gg in range(B_GROUP):
            blk = grp * B_GROUP + gg
            qh_ref[0, 0, grp, gg * TQ:(gg + 1) * TQ, :] = yq[:, blk * LANE:(blk + 1) * LANE].astype(BF16)
    kt_ref[0] = lax.dot_general(wk_ref[...], hb, _NT, preferred_element_type=F32)
    v_ref[0] = jnp.dot(hb, wv_ref[...], preferred_element_type=F32)


def _b_proj(x, g, wq, wk, wv):
    b, s, d = x.shape
    nq = s // TQ
    ngrp = B_KV_HEADS * 2
    kd = B_SIZES[1]
    full = lambda a: pl.BlockSpec(a.shape, lambda i, j: (0, 0))
    return pl.pallas_call(
        _b_proj_kernel,
        grid=(b, nq),
        in_specs=[pl.BlockSpec((1, TQ, d), lambda i, j: (i, j, 0)), pl.BlockSpec((1, d), lambda i, j: (0, 0)),
                  full(wq), full(wk), full(wv)],
        out_specs=[pl.BlockSpec((1, 1, ngrp, B_GROUP * TQ, LANE), lambda i, j: (i, j, 0, 0, 0)),
                   pl.BlockSpec((1, kd, TQ), lambda i, j: (i, 0, j)),
                   pl.BlockSpec((1, TQ, kd), lambda i, j: (i, j, 0))],
        out_shape=[jax.ShapeDtypeStruct((b, nq, ngrp, B_GROUP * TQ, LANE), BF16),
                   jax.ShapeDtypeStruct((b, kd, s), F32), jax.ShapeDtypeStruct((b, s, kd), F32)],
        compiler_params=_cparams(("parallel", "parallel")),
        name="b_proj",
    )(x, g.reshape(1, d), wq, wk, wv)


def _swap_halves(x, half):
    n = x.shape[-1]
    lane = lax.broadcasted_iota(I32, x.shape, x.ndim - 1)
    fwd = pltpu.roll(x, n - half, x.ndim - 1)
    bwd = pltpu.roll(x, half, x.ndim - 1)
    return jnp.where((lane // half) % 2 == 0, fwd, bwd)


def _c_proj_kernel(x_ref, g_ref, wd_ref, qn_ref, wuq_ref, kvn_ref, wuk_ref, wuv_ref,
                   cq_ref, sq_ref, ck_ref, sk_ref, q_o, k_o, v_o, ckv_o, kpet_o):
    hb = _rms(x_ref[0], g_ref[...]).astype(BF16)
    down = jnp.dot(hb, wd_ref[...], preferred_element_type=F32)
    cq = down[:, :Q_LORA]
    ckv = down[:, Q_LORA:Q_LORA + KV_LORA]
    kp = down[:, Q_LORA + KV_LORA:]
    qf = jnp.dot(_rms(cq, qn_ref[...]).astype(BF16), wuq_ref[...], preferred_element_type=F32)
    half = ROPE_DIM // 2
    cq_t, sq_t = cq_ref[...], sq_ref[...]
    for h in range(C_HEADS):
        x = qf[:, h * LANE:(h + 1) * LANE]
        x = x * cq_t + _swap_halves(x, half) * sq_t
        q_o[0, :, h * LANE:(h + 1) * LANE] = (x * C_SCALE).astype(BF16)
    ckv_n = _rms(ckv, kvn_ref[...])
    ckv_o[0] = ckv_n
    kp = kp * ck_ref[...] + _swap_halves(kp, half) * sk_ref[...]
    kpet_o[0] = kp.T[:ROPE_DIM]
    lane = lax.broadcasted_iota(I32, kp.shape, 1)
    k_tail = pltpu.roll(kp, NOPE_DIM, 1) + jnp.where(lane == C_COL_SHIFT, 1.0, 0.0)
    v_tail = jnp.where(lane == V_DIM, 1.0, 0.0)
    cb = ckv_n.astype(BF16)
    kn = jnp.dot(cb, wuk_ref[...], preferred_element_type=F32)
    vv = jnp.dot(cb, wuv_ref[...], preferred_element_type=F32)
    for h in range(C_HEADS):
        k_o[0, :, h * LANE:(h + 1) * LANE] = (kn[:, h * LANE:(h + 1) * LANE] + k_tail).astype(BF16)
        v_o[0, :, h * LANE:(h + 1) * LANE] = (vv[:, h * LANE:(h + 1) * LANE] + v_tail).astype(BF16)


def _c_proj(x, g, w_down, q_norm, w_uq_aug, kv_norm, w_uk_aug, w_uv_aug, tabs, tm):
    b, s, d = x.shape
    hw = C_HEADS * LANE
    full = lambda a: pl.BlockSpec(a.shape, lambda i, j: (0, 0))
    rows = lambda n: pl.BlockSpec((1, tm, n), lambda i, j: (i, j, 0))
    pos = pl.BlockSpec((tm, LANE), lambda i, j: (j, 0))
    qn2, kvn2 = q_norm.reshape(1, -1), kv_norm.reshape(1, -1)
    return pl.pallas_call(
        _c_proj_kernel,
        grid=(b, s // tm),
        in_specs=[rows(d), pl.BlockSpec((1, d), lambda i, j: (0, 0)), full(w_down), full(qn2), full(w_uq_aug),
                  full(kvn2), full(w_uk_aug), full(w_uv_aug), pos, pos, pos, pos],
        out_specs=[rows(hw), rows(hw), rows(hw), rows(KV_LORA),
                   pl.BlockSpec((1, ROPE_DIM, tm), lambda i, j: (i, 0, j))],
        out_shape=[jax.ShapeDtypeStruct((b, s, hw), BF16)] * 3
        + [jax.ShapeDtypeStruct((b, s, KV_LORA), F32), jax.ShapeDtypeStruct((b, ROPE_DIM, s), F32)],
        compiler_params=_cparams(("parallel", "parallel")),
        name="c_proj",
    )(x, g.reshape(1, d), w_down, qn2, w_uq_aug, kvn2, w_uk_aug, w_uv_aug, *tabs)


def _rope_tables(positions):
    half = ROPE_DIM // 2
    inv = ROPE_THETA ** (-np.arange(half, dtype=np.float32) / np.float32(half))
    ang = np.asarray(positions, np.float32)[:, None] * inv[None, :]
    c, s = np.cos(ang).astype(np.float32), np.sin(ang).astype(np.float32)
    n = ang.shape[0]

    def place(first):
        cos = np.ones((n, LANE), np.float32)
        sin = np.zeros((n, LANE), np.float32)
        cos[:, first:first + ROPE_DIM] = np.concatenate([c, c], axis=-1)
        sin[:, first:first + ROPE_DIM] = np.concatenate([-s, s], axis=-1)
        return jnp.asarray(cos), jnp.asarray(sin)

    return place(NOPE_DIM) + place(0)


def _silu(x):
    return x * (1.0 / (1.0 + jnp.exp(-x)))


def _ffn_prompt_kernel(x_ref, g_ref, wg_ref, wv_ref, cw_ref, wd_ref, gf_ref,
                       y_ref, st_ref, carry_ref, acc_ref, *, nchunk, final_norm):
    t = pl.program_id(1)
    nt = pl.num_programs(1)

    @pl.when(t == 0)
    def _():
        carry_ref[...] = jnp.zeros_like(carry_ref)

    x = x_ref[0]
    tm = x.shape[0]
    hb = _rms(x, g_ref[...]).astype(BF16)
    acc_ref[...] = x
    row = lax.broadcasted_iota(I32, (tm, FFN_CK), 0)

    def conv_part(c, part, w_ref):
        u = jnp.dot(hb, w_ref[c], preferred_element_type=F32)
        idx = part * nchunk + c
        prev = carry_ref[idx]
        p1, p2 = prev[7:8], prev[6:7]
        u1 = jnp.where(row == 0, p1, pltpu.roll(u, 1, 0))
        u2 = jnp.where(row == 0, p2, jnp.where(row == 1, p1, pltpu.roll(u, 2, 0)))
        cw = cw_ref[idx]
        carry_ref[idx, 6:8, :] = u[tm - 2:tm]
        return cw[3:4] + cw[2:3] * u + cw[1:2] * u1 + cw[0:1] * u2

    def body(c, _):
        gate = conv_part(c, 0, wg_ref)
        val = conv_part(c, 1, wv_ref)
        act = (_silu(gate) * val).astype(BF16)
        acc_ref[...] += jnp.dot(act, wd_ref[c], preferred_element_type=F32)
        return 0

    lax.fori_loop(0, nchunk, body, 0, unroll=True)
    y = acc_ref[...]
    if final_norm:
        y = _rms(y, gf_ref[...])
    y_ref[0] = y

    @pl.when(t == nt - 1)
    def _():
        st_ref[0] = carry_ref[...]


def _ffn_weights(w_up, conv_w, conv_b, w_down):
    d, two_f = w_up.shape
    f = two_f // 2
    nc = f // FFN_CK
    wu = w_up.astype(BF16)
    wg = wu[:, :f].reshape(d, nc, FFN_CK).transpose(1, 0, 2)
    wv = wu[:, f:].reshape(d, nc, FFN_CK).transpose(1, 0, 2)
    wd = w_down.astype(BF16).reshape(nc, FFN_CK, d)
    cw = jnp.concatenate([conv_w, conv_b[None], jnp.zeros((8 - CONV_W - 1, two_f), F32)], axis=0)
    cw = cw.reshape(8, 2 * nc, FFN_CK).transpose(1, 0, 2)
    return wg, wv, wd, cw, nc


def _ffn_prompt(x, g, fw, g_final, final_norm, tm):
    b, s, d = x.shape
    wg, wv, wd, cw, nc = fw
    y, st = pl.pallas_call(
        functools.partial(_ffn_prompt_kernel, nchunk=nc, final_norm=final_norm),
        grid=(b, s // tm),
        in_specs=[pl.BlockSpec((1, tm, d), lambda i, t: (i, t, 0)), pl.BlockSpec((1, d), lambda i, t: (0, 0)),
                  pl.BlockSpec(wg.shape, lambda i, t: (0, 0, 0)), pl.BlockSpec(wv.shape, lambda i, t: (0, 0, 0)),
                  pl.BlockSpec(cw.shape, lambda i, t: (0, 0, 0)),
                  pl.BlockSpec(wd.shape, lambda i, t: (0, 0, 0)), pl.BlockSpec((1, d), lambda i, t: (0, 0))],
        out_specs=[pl.BlockSpec((1, tm, d), lambda i, t: (i, t, 0)),
                   pl.BlockSpec((1, 2 * nc, 8, FFN_CK), lambda i, t: (i, 0, 0, 0))],
        out_shape=[jax.ShapeDtypeStruct((b, s, d), F32), jax.ShapeDtypeStruct((b, 2 * nc, 8, FFN_CK), F32)],
        scratch_shapes=[pltpu.VMEM((2 * nc, 8, FFN_CK), F32), pltpu.VMEM((tm, d), F32)],
        compiler_params=_cparams(("parallel", "arbitrary")),
        name="ffn_prompt",
    )(x, g.reshape(1, d), wg, wv, cw, wd, g_final.reshape(1, d))
    st = st[:, :, 6:8, :].transpose(0, 2, 1, 3).reshape(b, 2, 2 * nc * FFN_CK)
    return y, st


def _ffn_sample_kernel(x_ref, g_ref, wg_ref, wv_ref, cw_ref, wd_ref, gf_ref, p0_ref, p1_ref,
                       y_ref, u_ref, acc_ref, *, nchunk, final_norm):
    x = x_ref[...]
    hb = _rms(x, g_ref[...]).astype(BF16)
    acc_ref[...] = x

    def conv_part(c, part, w_ref):
        u = jnp.dot(hb, w_ref[c], preferred_element_type=F32)
        idx = part * nchunk + c
        u_ref[idx] = u
        cw = cw_ref[idx]
        return cw[3:4] + cw[2:3] * u + cw[1:2] * p1_ref[idx] + cw[0:1] * p0_ref[idx]

    def body(c, _):
        gate = conv_part(c, 0, wg_ref)
        val = conv_part(c, 1, wv_ref)
        act = (_silu(gate) * val).astype(BF16)
        acc_ref[...] += jnp.dot(act, wd_ref[c], preferred_element_type=F32)
        return 0

    lax.fori_loop(0, nchunk, body, 0)
    y = acc_ref[...]
    if final_norm:
        y = _rms(y, gf_ref[...])
    y_ref[...] = y


def _ffn_sample(x, g, fw, g_final, final_norm, state):
    m, d = x.shape
    wg, wv, wd, cw, nc = fw
    st = state.reshape(m, 2, 2 * nc, FFN_CK).transpose(1, 2, 0, 3)
    full3 = lambda a: pl.BlockSpec(a.shape, lambda i: (0, 0, 0))
    full2 = lambda a: pl.BlockSpec(a.shape, lambda i: (0, 0))
    p0, p1 = st[0], st[1]
    y, u = pl.pallas_call(
        functools.partial(_ffn_sample_kernel, nchunk=nc, final_norm=final_norm),
        grid=(1,),
        in_specs=[full2(x), pl.BlockSpec((1, d), lambda i: (0, 0)), full3(wg), full3(wv), full3(cw), full3(wd),
                  pl.BlockSpec((1, d), lambda i: (0, 0)), full3(p0), full3(p1)],
        out_specs=[full2(x), full3(p0)],
        out_shape=[jax.ShapeDtypeStruct((m, d), F32), jax.ShapeDtypeStruct(p0.shape, F32)],
        scratch_shapes=[pltpu.VMEM((m, d), F32)],
        compiler_params=_cparams(("arbitrary",)),
        name="ffn_sample",
    )(x, g.reshape(1, d), wg, wv, cw, wd, g_final.reshape(1, d), p0, p1)
    u = u.transpose(1, 0, 2).reshape(m, 2 * nc * FFN_CK)
    return y, jnp.stack([state[:, 1], u], axis=1)


def _sortable(s):
    b = lax.bitcast_convert_type(s + 0.0, I32)
    return jnp.where(b < 0, b ^ 0x7FFFFFFF, b)


_KEY_NEG_INF = int(np.int32(np.array(-np.inf, np.float32).view(np.int32)) ^ np.int32(0x7FFFFFFF))


def _lane_total(part):
    return jnp.dot(part.astype(BF16), jnp.ones((LANE, LANE), BF16), preferred_element_type=F32)


def _count(key_ref, nchunks, rows, pred):
    kc = key_ref.shape[2]

    def body(c, acc):
        hit = jnp.where(pred(key_ref[c]), 1.0, 0.0)
        for t in range(kc // LANE):
            acc = acc + hit[:, t * LANE:(t + 1) * LANE]
        return acc

    return _lane_total(lax.fori_loop(0, nchunks, body, jnp.zeros((rows, LANE), F32)))


def _kth_largest(key_ref, nchunks, rows, k):
    kc = key_ref.shape[2]

    def bit_step(i, carry):
        thr, cnt_thr = carry
        cand = thr + lax.shift_left(jnp.int32(1), 31 - i)
        cand_w = _rep(cand, kc)
        cnt = _count(key_ref, nchunks, rows, lambda key: key >= cand_w)
        ok = cnt >= k
        return jnp.where(ok, cand, thr), jnp.where(ok, cnt, cnt_thr)

    total = (nchunks * kc).astype(F32) if not isinstance(nchunks, int) else float(nchunks * kc)
    init = (jnp.full((rows, LANE), INT_MIN, I32), jnp.zeros((rows, LANE), F32) + total)
    return lax.fori_loop(0, 32, bit_step, init)


def _write_select_mask(key_ref, ma_ref, nchunks, rows, k, thr, cnt_thr, tri_ref):
    kc = key_ref.shape[2]
    thr_w = _rep(thr, kc)
    need_ties = jnp.max(jnp.where((cnt_thr > k) & (thr > _KEY_NEG_INF), 1.0, 0.0)) > 0.0

    @pl.when(jnp.logical_not(need_ties))
    def _():
        def body(c, _):
            ma_ref[c] = jnp.where(key_ref[c] >= thr_w, 0.0, MASKED)
            return 0
        lax.fori_loop(0, nchunks, body, 0)

    @pl.when(need_ties)
    def _():
        n_gt = _count(key_ref, nchunks, rows, lambda key: key > thr_w)
        room_w = _rep(k - n_gt, kc)
        ones = jnp.ones((kc, LANE), BF16)

        def body(c, seen):
            key = key_ref[c]
            eq = key == thr_w
            eqb = jnp.where(eq, 1.0, 0.0).astype(BF16)
            prefix = _rep(seen, kc) + jnp.dot(eqb, tri_ref[...], preferred_element_type=F32)
            keep = (key > thr_w) | (eq & (prefix <= room_w))
            ma_ref[c] = jnp.where(keep, 0.0, MASKED)
            return seen + jnp.dot(eqb, ones, preferred_element_type=F32)
        lax.fori_loop(0, nchunks, body, jnp.zeros((rows, LANE), F32))


def _tri(n):
    return jnp.asarray(np.triu(np.ones((n, n), np.float32)), BF16)


def _count_t(key_ref, nchunks, nq, pred):
    kc = key_ref.shape[1]

    def body(c, acc):
        hit = jnp.where(pred(key_ref[c]), 1.0, 0.0)
        return acc + jnp.sum(hit.reshape(kc // 8, 8, nq), axis=0)

    acc = lax.fori_loop(0, nchunks, body, jnp.zeros((8, nq), F32))
    return jnp.sum(acc, axis=0, keepdims=True)


def _kth_largest_t(key_ref, nchunks, nq, k):
    kc = key_ref.shape[1]

    def bit_step(i, carry):
        thr, cnt_thr = carry
        cand = thr + lax.shift_left(jnp.int32(1), 31 - i)
        cnt = _count_t(key_ref, nchunks, nq, lambda key: key >= cand)
        ok = cnt >= k
        return jnp.where(ok, cand, thr), jnp.where(ok, cnt, cnt_thr)

    init = (jnp.full((1, nq), INT_MIN, I32), jnp.zeros((1, nq), F32) + (nchunks * kc).astype(F32))
    return lax.fori_loop(0, 32, bit_step, init)


def _write_select_mask_t(key_ref, ma_ref, nchunks, nq, k, thr, cnt_thr, tril_ref):
    kc = key_ref.shape[1]
    need_ties = jnp.max(jnp.where((cnt_thr > k) & (thr > _KEY_NEG_INF), 1.0, 0.0)) > 0.0

    @pl.when(jnp.logical_not(need_ties))
    def _():
        def body(c, _):
            ma_ref[c] = jnp.where(key_ref[c] >= thr, 0.0, MASKED).T
            return 0
        lax.fori_loop(0, nchunks, body, 0)

    @pl.when(need_ties)
    def _():
        room = k - _count_t(key_ref, nchunks, nq, lambda key: key > thr)

        def body(c, seen):
            key = key_ref[c]
            eq = key == thr
            eqf = jnp.where(eq, 1.0, 0.0)
            prefix = seen + jnp.dot(tril_ref[...], eqf.astype(BF16), preferred_element_type=F32)
            keep = (key > thr) | (eq & (prefix <= room))
            ma_ref[c] = jnp.where(keep, 0.0, MASKED).T
            return seen + jnp.sum(jnp.sum(eqf.reshape(kc // 8, 8, nq), axis=0), axis=0, keepdims=True)
        lax.fori_loop(0, nchunks, body, jnp.zeros((1, nq), F32))


def _tril(n):
    return jnp.asarray(np.tril(np.ones((n, n), np.float32)), BF16)


def _ones_rows(n_ones, shape):
    return jnp.where(lax.broadcasted_iota(I32, shape, 0) < n_ones, 1.0, 0.0).astype(BF16)


def _ones_cols(n_ones, shape):
    return jnp.where(lax.broadcasted_iota(I32, shape, 1) < n_ones, 1.0, 0.0).astype(BF16)


def _causal_two_pass(qb, n_groups, rows, qa_ref, m_ref, acc_ref, logits_at, pv_at, tile_at, add_at, col_shift):
    n_stack = rows // TQ
    lane = lax.broadcasted_iota(I32, (rows, LANE), 1)

    def logits(g, c, off, w):
        s = logits_at(g, off, qa_ref[g])
        if add_at is not None:
            s = (s.reshape(n_stack, TQ, KC) + add_at(c)[None]).reshape(rows, KC)
        tile = tile_at(g, w) if w is not None else None
        return s if tile is None else s + tile

    def sweep(step):
        def far(c, _):
            step(c, None)
            return 0
        lax.fori_loop(0, jnp.maximum(qb - 1, 0), far, 0)

        @pl.when(qb >= 1)
        def _():
            step(qb - 1, 0)
        step(qb, 1)

    m_ref[...] = jnp.full(m_ref.shape, -jnp.inf, F32)

    def step_max(c, w):
        off = pl.multiple_of(c * KC, KC)
        for g in range(n_groups):
            s = logits(g, c, off, w)
            mx = s[:, :LANE]
            for t in range(1, KC // LANE):
                mx = jnp.maximum(mx, s[:, t * LANE:(t + 1) * LANE])
            m_ref[g] = jnp.maximum(m_ref[g], mx)

    sweep(step_max)
    for g in range(n_groups):
        m = jnp.max(m_ref[g], axis=-1, keepdims=True)
        qa_ref[g] = jnp.where(lane == col_shift, jnp.broadcast_to(-m, (rows, LANE)).astype(BF16), qa_ref[g])
    acc_ref[...] = jnp.zeros(acc_ref.shape, F32)

    def step_acc(c, w):
        off = pl.multiple_of(c * KC, KC)
        for g in range(n_groups):
            p = jnp.exp(logits(g, c, off, w)).astype(BF16)
            acc_ref[g] += pv_at(g, off, p)

    sweep(step_acc)


def _far_cols(far_ref, first, n, lane):
    hi = jnp.concatenate([jnp.full((TQ, LANE), far_ref[2 * (first + i)], F32) for i in range(n)], axis=0)
    lo = jnp.concatenate([jnp.full((TQ, LANE), far_ref[2 * (first + i) + 1], F32) for i in range(n)], axis=0)
    return jnp.where(lane == COL_BIAS_HI, hi, jnp.where(lane == COL_BIAS_LO, lo, 0.0))


_NT = (((1,), (1,)), ((), ()))


def _a_prompt_kernel(far_ref, qit_ref, kwt_ref, kw_ref, q_ref, kt_ref, vt_ref, tile_ref, tril_ref,
                     o_ref, key_ref, ma_ref, qip_ref, qa_ref, m_ref, acc_ref, *, topk):
    qb = pl.program_id(1)
    rows = A_GROUP * TQ
    zpad = jnp.zeros((LANE - IDX_DIM, TQ), BF16)
    for h in range(IDX_HEADS):
        qip_ref[h] = jnp.concatenate([qit_ref[0, h * IDX_DIM:(h + 1) * IDX_DIM, :], zpad], axis=0)
    wt = kwt_ref[0, IDX_DIM:IDX_DIM + IDX_HEADS, :] * (IDX_HEADS ** -0.5)

    def score_t(c):
        off = pl.multiple_of(c * KC, KC)
        kb = kw_ref[0, pl.ds(off, KC), :].astype(BF16)
        s = None
        for h in range(IDX_HEADS):
            d = jnp.maximum(jnp.dot(kb, qip_ref[h], preferred_element_type=F32), 0.0) * wt[h:h + 1]
            s = d if s is None else s + d
        return s

    def score_body(c, _):
        key_ref[c] = _sortable(score_t(c))
        return 0

    lax.fori_loop(0, qb, score_body, 0)
    causal = (lax.broadcasted_iota(I32, (KC, TQ), 0) <= lax.broadcasted_iota(I32, (KC, TQ), 1))
    key_ref[qb] = _sortable(jnp.where(causal, score_t(qb), -jnp.inf))
    thr, cnt_thr = _kth_largest_t(key_ref, qb + 1, TQ, topk)
    _write_select_mask_t(key_ref, ma_ref, qb + 1, TQ, topk, thr, cnt_thr, tril_ref)
    lane = lax.broadcasted_iota(I32, (rows, LANE), 1)
    for kv in range(A_KV_HEADS):
        q = q_ref[0, 0, kv * A_GROUP:(kv + 1) * A_GROUP].reshape(rows, LANE)
        qa_ref[kv] = jnp.where(lane < HEAD_DIM, q, _far_cols(far_ref, kv * A_GROUP, A_GROUP, lane).astype(BF16))
    k_ones = _ones_rows(3, (LANE - HEAD_DIM, KC))
    v_ones = _ones_rows(1, (LANE - HEAD_DIM, KC))

    def logits_at(kv, off, qa):
        kt = jnp.concatenate([kt_ref[0, kv * HEAD_DIM:(kv + 1) * HEAD_DIM, pl.ds(off, KC)].astype(BF16), k_ones], axis=0)
        return jnp.dot(qa, kt, preferred_element_type=F32)

    def pv_at(kv, off, p):
        vt = jnp.concatenate([vt_ref[0, kv * HEAD_DIM:(kv + 1) * HEAD_DIM, pl.ds(off, KC)].astype(BF16), v_ones], axis=0)
        return lax.dot_general(p, vt, _NT, preferred_element_type=F32)

    _causal_two_pass(qb, A_KV_HEADS, rows, qa_ref, m_ref, acc_ref, logits_at, pv_at,
                     lambda kv, w: tile_ref[kv, w], lambda c: ma_ref[c], COL_SHIFT)
    for kv in range(A_KV_HEADS):
        acc = acc_ref[kv]
        o = acc[:, :HEAD_DIM] / acc[:, HEAD_DIM:HEAD_DIM + 1]
        for g in range(A_GROUP):
            h = kv * A_GROUP + g
            o_ref[0, :, h * HEAD_DIM:(h + 1) * HEAD_DIM] = o[g * TQ:(g + 1) * TQ].astype(o_ref.dtype)


def _a_prompt_attn(far, qit, kwt, kw, qh, kt, vt, tiles, topk):
    b, nq = qh.shape[0], qh.shape[1]
    s = nq * TQ
    nc = s // KC
    rows = A_GROUP * TQ
    kvd = A_KV_HEADS * HEAD_DIM
    grid_spec = pltpu.PrefetchScalarGridSpec(
        num_scalar_prefetch=1,
        grid=(b, nq),
        in_specs=[
            pl.BlockSpec((1, IDX_HEADS * IDX_DIM, TQ), lambda i, j, f: (i, 0, j)),
            pl.BlockSpec((1, LANE, TQ), lambda i, j, f: (i, 0, j)),
            pl.BlockSpec((1, s, LANE), lambda i, j, f: (i, 0, 0)),
            pl.BlockSpec((1, 1, A_HEADS, TQ, LANE), lambda i, j, f: (i, j, 0, 0, 0)),
            pl.BlockSpec((1, kvd, s), lambda i, j, f: (i, 0, 0)),
            pl.BlockSpec((1, kvd, s), lambda i, j, f: (i, 0, 0)),
            pl.BlockSpec(tiles.shape, lambda i, j, f: (0, 0, 0, 0)),
            pl.BlockSpec((KC, KC), lambda i, j, f: (0, 0)),
        ],
        out_specs=pl.BlockSpec((1, TQ, A_HEADS * HEAD_DIM), lambda i, j, f: (i, j, 0)),
        scratch_shapes=[pltpu.VMEM((nc, KC, TQ), I32), pltpu.VMEM((nc, TQ, KC), F32),
                        pltpu.VMEM((IDX_HEADS, LANE, TQ), BF16), pltpu.VMEM((A_KV_HEADS, rows, LANE), BF16),
                        pltpu.VMEM((A_KV_HEADS, rows, LANE), F32), pltpu.VMEM((A_KV_HEADS, rows, LANE), F32)],
    )
    return pl.pallas_call(
        functools.partial(_a_prompt_kernel, topk=topk),
        grid_spec=grid_spec,
        out_shape=jax.ShapeDtypeStruct((b, s, A_HEADS * HEAD_DIM), BF16),
        compiler_params=_cparams(("parallel", "arbitrary")),
        name="a_prompt_attn",
    )(far, qit, kwt, kw, qh, kt, vt, tiles, _tril(KC))


def _b_prompt_kernel(far_ref, q_ref, kt_ref, v_ref, tile_ref, o0_ref, o1_ref, qa_ref, m_ref, acc_ref):
    qb = pl.program_id(1)
    hd = 2 * HEAD_DIM
    rows = B_GROUP * TQ
    ngrp = B_KV_HEADS * 2
    lane = lax.broadcasted_iota(I32, (rows, LANE), 1)
    for grp in range(ngrp):
        qa_ref[grp] = jnp.where(lane < HEAD_DIM, q_ref[0, 0, grp],
                                _far_cols(far_ref, grp * B_GROUP, B_GROUP, lane).astype(BF16))
    k_ones = _ones_rows(3, (LANE - HEAD_DIM, KC))
    v_ones = _ones_cols(1, (KC, LANE))

    def logits_at(grp, off, qa):
        kt = jnp.concatenate([kt_ref[0, grp * HEAD_DIM:(grp + 1) * HEAD_DIM, pl.ds(off, KC)].astype(BF16), k_ones], axis=0)
        return jnp.dot(qa, kt, preferred_element_type=F32)

    def pv_at(grp, off, p):
        kv = grp // 2
        v = jnp.concatenate([v_ref[0, pl.ds(off, KC), kv * hd:(kv + 1) * hd].astype(BF16), v_ones], axis=1)
        return jnp.dot(p, v, preferred_element_type=F32)

    _causal_two_pass(qb, ngrp, rows, qa_ref, m_ref, acc_ref, logits_at, pv_at,
                     lambda grp, w: tile_ref[grp, w], None, COL_SHIFT)
    for grp in range(ngrp):
        kv, j = grp // 2, grp % 2
        acc = acc_ref[grp]
        o = acc[:, :hd] / acc[:, hd:hd + 1]
        o_ref = o0_ref if j == 0 else o1_ref
        for g in range(B_GROUP):
            h = kv * B_GROUP + g
            o_ref[0, :, h * hd:(h + 1) * hd] = o[g * TQ:(g + 1) * TQ]


def _b_prompt_attn(far, qh, kt, v, tiles):
    b, nq = qh.shape[0], qh.shape[1]
    s = nq * TQ
    hd = 2 * HEAD_DIM
    ngrp = B_KV_HEADS * 2
    rows = B_GROUP * TQ
    grid_spec = pltpu.PrefetchScalarGridSpec(
        num_scalar_prefetch=1,
        grid=(b, nq),
        in_specs=[
            pl.BlockSpec((1, 1, ngrp, rows, LANE), lambda i, j, f: (i, j, 0, 0, 0)),
            pl.BlockSpec((1, ngrp * HEAD_DIM, s), lambda i, j, f: (i, 0, 0)),
            pl.BlockSpec((1, s, B_KV_HEADS * hd), lambda i, j, f: (i, 0, 0)),
            pl.BlockSpec(tiles.shape, lambda i, j, f: (0, 0, 0, 0)),
        ],
        out_specs=[pl.BlockSpec((1, TQ, B_HEADS * hd), lambda i, j, f: (i, j, 0))] * 2,
        scratch_shapes=[pltpu.VMEM((ngrp, rows, LANE), BF16), pltpu.VMEM((ngrp, rows, LANE), F32),
                        pltpu.VMEM((ngrp, rows, 2 * LANE), F32)],
    )
    return pl.pallas_call(
        _b_prompt_kernel,
        grid_spec=grid_spec,
        out_shape=[jax.ShapeDtypeStruct((b, s, B_HEADS * hd), F32)] * 2,
        compiler_params=_cparams(("parallel", "arbitrary")),
        name="b_prompt_attn",
    )(far, qh, kt, v, tiles)


def _c_prompt_kernel(q_ref, k_ref, v_ref, tile_ref, o_ref, qa_ref, m_ref, acc_ref):
    qb = pl.program_id(1)
    for h in range(C_HEADS):
        qa_ref[h] = q_ref[0, :, h * LANE:(h + 1) * LANE]

    def logits_at(h, off, qa):
        return lax.dot_general(qa, k_ref[0, pl.ds(off, KC), h * LANE:(h + 1) * LANE], _NT, preferred_element_type=F32)

    def pv_at(h, off, p):
        return jnp.dot(p, v_ref[0, pl.ds(off, KC), h * LANE:(h + 1) * LANE], preferred_element_type=F32)

    _causal_two_pass(qb, C_HEADS, TQ, qa_ref, m_ref, acc_ref, logits_at, pv_at,
                     lambda h, w: tile_ref[...] if w == 1 else None, None, C_COL_SHIFT)
    for h in range(C_HEADS):
        acc = acc_ref[h]
        o_ref[0, :, h * V_DIM:(h + 1) * V_DIM] = (acc[:, :V_DIM] / acc[:, V_DIM:V_DIM + 1]).astype(o_ref.dtype)


def _c_prompt_attn(q, k, v, tile):
    b, s, hw = q.shape
    nq = s // TQ
    return pl.pallas_call(
        _c_prompt_kernel,
        grid=(b, nq),
        in_specs=[pl.BlockSpec((1, TQ, hw), lambda i, j: (i, j, 0)),
                  pl.BlockSpec((1, s, hw), lambda i, j: (i, 0, 0)),
                  pl.BlockSpec((1, s, hw), lambda i, j: (i, 0, 0)),
                  pl.BlockSpec((TQ, KC), lambda i, j: (0, 0))],
        out_specs=pl.BlockSpec((1, TQ, C_HEADS * V_DIM), lambda i, j: (i, j, 0)),
        out_shape=jax.ShapeDtypeStruct((b, s, C_HEADS * V_DIM), BF16),
        scratch_shapes=[pltpu.VMEM((C_HEADS, TQ, LANE), BF16), pltpu.VMEM((C_HEADS, TQ, LANE), F32),
                        pltpu.VMEM((C_HEADS, TQ, LANE), F32)],
        compiler_params=_cparams(("parallel", "arbitrary")),
        name="c_prompt_attn",
    )(q, k, v, tile)


def _bias_tiles(rel_table, cols):
    i = np.arange(TQ)[:, None]
    j = np.arange(KC)[None, :]
    dist = np.stack([i - j + KC, i - j])
    assert np.all(_rel_bucket_np(np.arange(KC + 1, 16 * KC)) == REL_BUCKETS - 1)
    tab = rel_table.astype(F32)[:, np.asarray(cols)]
    far = tab[REL_BUCKETS - 1]
    hi = far.astype(BF16).astype(F32)
    lo = (far - hi).astype(BF16).astype(F32)
    tiles = jnp.transpose(_bias_lookup(tab, dist), (3, 0, 1, 2)) - (hi + lo)[:, None, None, None]
    tiles = jnp.where(jnp.asarray((dist >= 0)[None]), tiles, MASKED)
    return tiles, jnp.stack([hi, lo], axis=1).reshape(-1)


def _page_specs(block_tail, slot, n_pages, pps):
    specs = []
    for p in range(pps):
        def imap(i, g, pt, p=p):
            return (slot, pt[i * n_pages + g * pps + p]) + (0,) * len(block_tail)
        specs.append(pl.BlockSpec((None, None) + block_tail, imap))
    return specs


def _a_idx_kernel(pt_ref, qi_ref, wi_ref, *refs, pps):
    page_refs, o_ref = refs[:pps], refs[pps]
    qi = qi_ref[0]
    wi = wi_ref[0] * (IDX_HEADS ** -0.5)
    outs = []
    for p in range(pps):
        d = jnp.dot(qi, page_refs[p][...].astype(BF16), preferred_element_type=F32)
        outs.append(jnp.sum(jnp.maximum(d, 0.0) * wi, axis=0, keepdims=True))
    o_ref[0] = jnp.concatenate(outs, axis=-1)


def _a_idx_scores(page_table, qi, wi, pool_idx_t, slot):
    db = qi.shape[0]
    n_pages = page_table.shape[1]
    pps = IDX_PAGES_PER_STEP
    grid_spec = pltpu.PrefetchScalarGridSpec(
        num_scalar_prefetch=1,
        grid=(db, n_pages // pps),
        in_specs=[pl.BlockSpec((1, 2 * IDX_HEADS, IDX_DIM), lambda i, g, pt: (i, 0, 0)),
                  pl.BlockSpec((1, 2 * IDX_HEADS, 1), lambda i, g, pt: (i, 0, 0))]
        + _page_specs((IDX_DIM, PAGE_SIZE), slot, n_pages, pps),
        out_specs=pl.BlockSpec((1, 1, pps * PAGE_SIZE), lambda i, g, pt: (i, 0, g)),
    )
    return pl.pallas_call(
        functools.partial(_a_idx_kernel, pps=pps),
        grid_spec=grid_spec,
        out_shape=jax.ShapeDtypeStruct((db, 1, n_pages * PAGE_SIZE), F32),
        compiler_params=_cparams(("parallel", "arbitrary")),
        name="a_idx_scores",
    )(page_table.reshape(-1), qi, wi, *([pool_idx_t] * pps))


def _a_select_kernel(sp_ref, qi_ref, ki_ref, wi_ref, tri_ref, ma_ref, key_ref, *, topk, n_past):
    rows = sp_ref.shape[0]
    nch = n_past // KC
    for c in range(nch):
        key_ref[c] = _sortable(sp_ref[:, c * KC:(c + 1) * KC])
    d = jnp.sum(qi_ref[...] * ki_ref[...], axis=-1)
    s_new = jnp.sum(jnp.maximum(d, 0.0) * (wi_ref[...] * (IDX_HEADS ** -0.5)), axis=-1, keepdims=True)
    lane = lax.broadcasted_iota(I32, (rows, KC), 1)
    key_ref[nch] = _sortable(jnp.where(lane == 0, s_new, -jnp.inf))
    thr, cnt_thr = _kth_largest(key_ref, nch + 1, rows, topk)
    _write_select_mask(key_ref, ma_ref, nch + 1, rows, topk, thr, cnt_thr, tri_ref)


def _a_select(scores_past, qi, ki, wi, topk):
    db, n_past = scores_past.shape
    nch = n_past // KC
    args = (scores_past, qi, ki, wi, _tri(KC))
    return pl.pallas_call(
        functools.partial(_a_select_kernel, topk=topk, n_past=n_past),
        grid=(1,),
        in_specs=[pl.BlockSpec(a.shape, lambda i, nd=a.ndim: (0,) * nd) for a in args],
        out_specs=pl.BlockSpec((nch + 1, db, KC), lambda i: (0, 0, 0)),
        out_shape=jax.ShapeDtypeStruct((nch + 1, db, KC), F32),
        scratch_shapes=[pltpu.VMEM((nch + 1, db, KC), I32)],
        compiler_params=_cparams(("arbitrary",)),
        name="a_select",
    )(*args)


def _paged_attn_kernel(pt_ref, *refs, pps, k_feature_major, v_mode, n_interleave, has_mask, has_bias):
    n_kparts = len(k_feature_major)
    g = pl.program_id(1)
    ng = pl.num_programs(1)
    it = iter(refs)
    q_refs = [next(it) for _ in range(n_kparts)]
    kn_refs = [next(it) for _ in range(n_kparts)]
    vn_ref = next(it)
    an_ref = next(it)
    bias_ref = next(it) if has_bias else None
    mask_ref = next(it) if has_mask else None
    k_pages = [[next(it) for _ in range(pps)] for _ in range(n_kparts)]
    v_pages = k_pages[0] if v_mode == 'share0' else [next(it) for _ in range(pps)]
    o_ref, m_ref, l_ref, acc_ref = next(it), next(it), next(it), next(it)
    rows = acc_ref.shape[0]

    @pl.when(g == 0)
    def _():
        s_new = an_ref[0]
        for q_ref, kn_ref in zip(q_refs, kn_refs):
            s_new = s_new + jnp.sum(q_ref[0].astype(F32) * kn_ref[0], axis=-1, keepdims=True)
        m_ref[...] = s_new
        l_ref[...] = jnp.ones_like(l_ref)
        acc_ref[...] = jnp.broadcast_to(vn_ref[0], acc_ref.shape)

    parts = []
    shared = []
    for p in range(pps):
        s = None
        for part in range(n_kparts):
            kb = k_pages[part][p][...].astype(BF16)
            if part == 0 and v_mode == 'share0':
                shared.append(kb)
            if k_feature_major[part]:
                d = jnp.dot(q_refs[part][0], kb, preferred_element_type=F32)
            else:
                d = lax.dot_general(q_refs[part][0], kb, _NT, preferred_element_type=F32)
            s = d if s is None else s + d
        parts.append(s)
    s = jnp.concatenate(parts, axis=-1)
    if has_bias:
        s = s + bias_ref[...]
    if has_mask:
        s = s + mask_ref[0]
    m = m_ref[...]
    m_new = jnp.maximum(m, jnp.max(s, axis=-1, keepdims=True))
    alpha = jnp.exp(m - m_new)
    p_all = jnp.exp(s - m_new)
    l_ref[...] = alpha * l_ref[...] + jnp.sum(p_all, axis=-1, keepdims=True)
    pb = p_all.astype(BF16)
    acc = alpha * acc_ref[...]
    if v_mode == 'interleaved':
        row_kv = lax.broadcasted_iota(I32, acc.shape, 0) // (rows // n_interleave)
    for p in range(pps):
        pp = pb[:, p * PAGE_SIZE:(p + 1) * PAGE_SIZE]
        if v_mode == 'share0':
            acc = acc + jnp.dot(pp, shared[p], preferred_element_type=F32)
        elif v_mode == 'feature_major':
            acc = acc + lax.dot_general(pp, v_pages[p][...].astype(BF16), _NT, preferred_element_type=F32)
        else:
            upd = jnp.zeros(acc.shape, F32)
            for kv in range(n_interleave):
                v_kv = v_pages[p][pl.ds(kv, PAGE_SIZE, stride=n_interleave), :].astype(BF16)
                upd = jnp.where(row_kv == kv, jnp.dot(pp, v_kv, preferred_element_type=F32), upd)
            acc = acc + upd
    acc_ref[...] = acc
    m_ref[...] = m_new

    @pl.when(g == ng - 1)
    def _():
        o_ref[0] = acc_ref[...] / l_ref[...]


def _paged_attn(page_table, slot, q_parts, knew_parts, v_new, add_new, bias, mask, k_pools, k_feature_major,
                v_pool, v_mode, n_interleave=1):
    db, rows = q_parts[0].shape[:2]
    n_pages = page_table.shape[1]
    page_bytes = sum(int(np.prod(p.shape[2:])) * 4 for p in list(k_pools) + ([] if v_pool is None else [v_pool]))
    pps = n_pages
    while pps > 1 and (2 * pps * page_bytes > PAGE_VMEM_BUDGET or pps > MAX_PAGES_PER_STEP):
        pps //= 2
    dv = v_new.shape[-1]
    in_specs, args = [], []
    for q in q_parts:
        in_specs.append(pl.BlockSpec((1, rows, q.shape[-1]), lambda i, g, pt: (i, 0, 0)))
        args.append(q)
    for kn in knew_parts + [v_new]:
        in_specs.append(pl.BlockSpec((1,) + kn.shape[1:], lambda i, g, pt: (i, 0, 0)))
        args.append(kn)
    in_specs.append(pl.BlockSpec((1, rows, 1), lambda i, g, pt: (i, 0, 0)))
    args.append(add_new)
    if bias is not None:
        in_specs.append(pl.BlockSpec((rows, pps * PAGE_SIZE), lambda i, g, pt: (0, g)))
        args.append(bias)
    if mask is not None:
        in_specs.append(pl.BlockSpec((1, 1, pps * PAGE_SIZE), lambda i, g, pt: (i, 0, g)))
        args.append(mask)
    for pool in k_pools:
        in_specs += _page_specs(tuple(pool.shape[2:]), slot, n_pages, pps)
        args += [pool] * pps
    if v_mode != 'share0':
        in_specs += _page_specs(tuple(v_pool.shape[2:]), slot, n_pages, pps)
        args += [v_pool] * pps
    grid_spec = pltpu.PrefetchScalarGridSpec(
        num_scalar_prefetch=1,
        grid=(db, n_pages // pps),
        in_specs=in_specs,
        out_specs=pl.BlockSpec((1, rows, dv), lambda i, g, pt: (i, 0, 0)),
        scratch_shapes=[pltpu.VMEM((rows, 1), F32), pltpu.VMEM((rows, 1), F32), pltpu.VMEM((rows, dv), F32)],
    )
    return pl.pallas_call(
        functools.partial(_paged_attn_kernel, pps=pps, k_feature_major=tuple(k_feature_major), v_mode=v_mode,
                          n_interleave=n_interleave, has_mask=mask is not None, has_bias=bias is not None),
        grid_spec=grid_spec,
        out_shape=jax.ShapeDtypeStruct((db, rows, dv), F32),
        compiler_params=_cparams(("parallel", "arbitrary")),
        name="paged_attn",
    )(page_table.reshape(-1), *args)


def _c_post_kernel(o_ref, wuv_ref, wo_ref, x_ref, y_ref):
    parts = []
    for h in range(C_HEADS):
        parts.append(jnp.dot(o_ref[h].astype(BF16), wuv_ref[h], preferred_element_type=F32))
    o = jnp.concatenate(parts, axis=-1).astype(BF16)
    y_ref[...] = x_ref[...] + jnp.dot(o, wo_ref[...], preferred_element_type=F32)


def _c_post(o_lat, w_uv, w_o, x):
    args = (o_lat, w_uv, w_o, x)
    return pl.pallas_call(
        _c_post_kernel,
        grid=(1,),
        in_specs=[pl.BlockSpec(a.shape, lambda i, nd=a.ndim: (0,) * nd) for a in args],
        out_specs=pl.BlockSpec(x.shape, lambda i: (0, 0)),
        out_shape=jax.ShapeDtypeStruct(x.shape, F32),
        compiler_params=_cparams(("arbitrary",)),
        name="c_post",
    )(*args)


def _q_lat_kernel(qn_ref, wuk_ref, o_ref):
    for h in range(C_HEADS):
        o_ref[h] = lax.dot_general(qn_ref[h], wuk_ref[h], _NT, preferred_element_type=F32).astype(o_ref.dtype)


def _q_lat(qn, w_uk_h):
    args = (qn, w_uk_h)
    h, m = qn.shape[:2]
    return pl.pallas_call(
        _q_lat_kernel,
        grid=(1,),
        in_specs=[pl.BlockSpec(a.shape, lambda i, nd=a.ndim: (0,) * nd) for a in args],
        out_specs=pl.BlockSpec((h, m, KV_LORA), lambda i: (0, 0, 0)),
        out_shape=jax.ShapeDtypeStruct((h, m, KV_LORA), BF16),
        compiler_params=_cparams(("arbitrary",)),
        name="c_q_lat",
    )(*args)


def _blockdiag_q(q, n_blocks):
    db, nb, g, d = q.shape
    eye = jnp.eye(nb, dtype=q.dtype)
    return jnp.einsum('bkgd,kc->bkgcd', q, eye).reshape(db, nb * g, nb * d)


def _diag_blocks(o, n_blocks, g):
    db, rows, cols = o.shape
    e = cols // n_blocks
    o = o.reshape(db, n_blocks, g, n_blocks, e)
    return jnp.stack([o[:, k, :, k] for k in range(n_blocks)], axis=1)


def _feature_major_pages(pool):
    nd = pool.ndim
    t = jnp.transpose(pool, (0, 1) + tuple(range(3, nd)) + (2,))
    return t.reshape(pool.shape[0], pool.shape[1], -1, pool.shape[2])


def _mixer_a(xp, xs, g, w_in, w_o, rel_table, pool_k, pool_v, pool_i, slot, page_table):
    b, s, d = xp.shape
    db = xs.shape[0]
    wq, wk, wv, wqi, wki, wwi = [w.astype(BF16) for w in jnp.split(w_in, np.cumsum(A_SIZES)[:-1].tolist(), axis=-1)]
    wkw = jnp.pad(jnp.concatenate([wki, wwi], axis=1), ((0, 0), (0, LANE - IDX_DIM - IDX_HEADS)))
    wo = w_o.astype(BF16)
    cols = list(range(A_HEADS))
    q_scale, qi_scale = HEAD_DIM ** -0.5, IDX_DIM ** -0.5
    qh, qit, kt, vt, kw, kwt = _a_proj(xp, g, _pad_heads(wq * q_scale, A_HEADS, HEAD_DIM), wk.T, wv.T,
                                       (wqi * qi_scale).T, wkw)
    topk = min(TOPK_MAX, s // 4)
    tiles, far = _bias_tiles(rel_table, cols)
    tiles = tiles.reshape(A_KV_HEADS, A_GROUP, 2, TQ, KC).transpose(0, 2, 1, 3, 4).reshape(A_KV_HEADS, 2, A_GROUP * TQ, KC)
    attn = _a_prompt_attn(far, qit, kwt, kw, qh, kt, vt, tiles, topk)
    yp = _out_proj(attn.reshape(b * s, -1), wo, xp.reshape(b * s, d), 512).reshape(b, s, d)
    rows_p = (kt.reshape(b, A_KV_HEADS, HEAD_DIM, s).transpose(0, 3, 1, 2),
              vt.reshape(b, A_KV_HEADS, HEAD_DIM, s).transpose(0, 3, 1, 2), kwt[:, :IDX_DIM].transpose(0, 2, 1))
    n_pages = page_table.shape[1]
    past = n_pages * PAGE_SIZE
    q, k, v, qi, kw = _norm_proj(xs.reshape(db, d), g, [wq, wk, wv, wqi, wkw], [BF16, F32, F32, BF16, F32],
                                 [q_scale, 1.0, 1.0, qi_scale, 1.0], db)
    ki, wi = kw[:, :IDX_DIM], kw[:, IDX_DIM:IDX_DIM + IDX_HEADS]
    qi3 = qi.reshape(db, IDX_HEADS, IDX_DIM)
    sc = _a_idx_scores(page_table, jnp.pad(qi3, ((0, 0), (0, IDX_HEADS), (0, 0))),
                       jnp.pad(wi.reshape(db, IDX_HEADS, 1), ((0, 0), (0, IDX_HEADS), (0, 0))),
                       _feature_major_pages(pool_i), slot)
    topk_s = min(TOPK_MAX, (past + 1) // 4)
    ma = _a_select(sc.reshape(db, past), qi3.astype(F32), ki.reshape(db, 1, IDX_DIM), wi, topk_s)
    mask = ma[:past // KC].transpose(1, 0, 2).reshape(db, 1, past)
    tab = rel_table.astype(F32)
    bias = _bias_lookup(tab, past - np.arange(past)).T
    add_new = tab[0][None, :, None] + ma[past // KC, :, 0][:, None, None]
    qm = _blockdiag_q(q.reshape(db, A_KV_HEADS, A_GROUP, HEAD_DIM), A_KV_HEADS)
    o = _paged_attn(page_table, slot, [qm], [k.reshape(db, 1, -1)], v.reshape(db, 1, -1), add_new, bias, mask,
                    [_feature_major_pages(pool_k)], [True], _feature_major_pages(pool_v), 'feature_major')
    o = _diag_blocks(o, A_KV_HEADS, A_GROUP).reshape(db, A_HEADS * HEAD_DIM)
    ys = _out_proj(o, wo, xs.reshape(db, d), db).reshape(db, 1, d)
    rows_s = (k.reshape(db, 1, A_KV_HEADS, HEAD_DIM), v.reshape(db, 1, A_KV_HEADS, HEAD_DIM), ki.reshape(db, 1, IDX_DIM))
    return yp, ys, rows_p, rows_s


def _mixer_b(xp, xs, g, w_in, lam_p, subln, w_o, rel_table, lam_init, pool_k, pool_v, slot, page_table):
    b, s, d = xp.shape
    db = xs.shape[0]
    hd = 2 * HEAD_DIM
    wq, wk, wv = [w.astype(BF16) for w in jnp.split(w_in, np.cumsum(B_SIZES)[:-1].tolist(), axis=-1)]
    wo = w_o.astype(BF16)
    ngrp = B_KV_HEADS * 2
    q_scale = HEAD_DIM ** -0.5
    cols = [2 * (kv * B_GROUP + gg) + j for kv in range(B_KV_HEADS) for j in range(2) for gg in range(B_GROUP)]
    wq_kjg = (wq * q_scale).reshape(d, B_KV_HEADS, B_GROUP, 2, HEAD_DIM).transpose(0, 1, 3, 2, 4).reshape(d, -1)
    qh, kt, v = _b_proj(xp, g, _pad_heads(wq_kjg, B_HEADS * 2, HEAD_DIM), wk.T, wv)
    tiles, far = _bias_tiles(rel_table, cols)
    tiles = tiles.reshape(ngrp, B_GROUP, 2, TQ, KC).transpose(0, 2, 1, 3, 4).reshape(ngrp, 2, B_GROUP * TQ, KC)
    o0, o1 = _b_prompt_attn(far, qh, kt, v, tiles)
    yp = _b_post(o0.reshape(b * s, -1), o1.reshape(b * s, -1), lam_p, subln, wo, xp.reshape(b * s, d),
                 lam_init, 512).reshape(b, s, d)
    rows_p = (kt.reshape(b, B_KV_HEADS, 2, HEAD_DIM, s).transpose(0, 4, 1, 2, 3), v.reshape(b, s, B_KV_HEADS, hd))
    n_pages = page_table.shape[1]
    past = n_pages * PAGE_SIZE
    q, k, v = _norm_proj(xs.reshape(db, d), g, [wq, wk, wv], [BF16, F32, F32], [q_scale, 1.0, 1.0], db)
    tab = rel_table.astype(F32)
    colr = np.asarray(cols).reshape(B_KV_HEADS, 2, B_GROUP).transpose(0, 2, 1).reshape(-1)
    bias = _bias_lookup(tab[:, colr], past - np.arange(past)).T
    add_new = jnp.broadcast_to(tab[0][colr][None, :, None], (db, len(cols), 1))
    q5 = q.reshape(db, B_KV_HEADS, B_GROUP, 2, HEAD_DIM)
    eye = jnp.eye(ngrp, dtype=q.dtype).reshape(B_KV_HEADS, 2, B_KV_HEADS, 2)
    qm = jnp.einsum('bkgjd,kjcm->bkgjcmd', q5, eye).reshape(db, ngrp * B_GROUP, ngrp * HEAD_DIM)
    v_new = jnp.repeat(v.reshape(db, B_KV_HEADS, hd), B_GROUP * 2, axis=1)
    pool_v_rows = pool_v.reshape(pool_v.shape[0], pool_v.shape[1], PAGE_SIZE * B_KV_HEADS, hd)
    o = _paged_attn(page_table, slot, [qm], [k.reshape(db, 1, -1)], v_new, add_new, bias, None,
                    [_feature_major_pages(pool_k)], [True], pool_v_rows, 'interleaved', B_KV_HEADS)
    o = o.reshape(db, B_KV_HEADS, B_GROUP, 2, hd)
    o0 = o[:, :, :, 0].reshape(db, B_HEADS * hd)
    o1 = o[:, :, :, 1].reshape(db, B_HEADS * hd)
    ys = _b_post(o0, o1, lam_p, subln, wo, xs.reshape(db, d), lam_init, db).reshape(db, 1, d)
    rows_s = (k.reshape(db, 1, B_KV_HEADS, 2, HEAD_DIM), v.reshape(db, 1, B_KV_HEADS, hd))
    return yp, ys, rows_p, rows_s


def _mixer_c(xp, xs, g, w_down, q_norm, w_uq, kv_norm, w_ukv, w_o, pool_ckv, pool_kpe, slot, page_table):
    b, s, d = xp.shape
    db = xs.shape[0]
    n_pages = page_table.shape[1]
    past = n_pages * PAGE_SIZE
    wd = jnp.pad(w_down.astype(BF16), ((0, 0), (0, Q_LORA + KV_LORA + LANE - w_down.shape[1])))
    wq_aug = _pad_heads(w_uq.astype(BF16), C_HEADS, C_QK)
    wkv = w_ukv.astype(BF16).reshape(KV_LORA, C_HEADS, NOPE_DIM + V_DIM)
    wuk_aug = _pad_heads(wkv[:, :, :NOPE_DIM].reshape(KV_LORA, -1), C_HEADS, NOPE_DIM)
    wuv_aug = _pad_heads(wkv[:, :, NOPE_DIM:].reshape(KV_LORA, -1), C_HEADS, V_DIM)
    wo = w_o.astype(BF16)
    q, k, v, ckv, kpet = _c_proj(xp, g, wd, q_norm, wq_aug, kv_norm, wuk_aug, wuv_aug, _rope_tables(np.arange(s)), TQ)
    i = np.arange(TQ)[:, None]
    j = np.arange(KC)[None, :]
    attn = _c_prompt_attn(q, k, v, jnp.asarray(np.where(i >= j, 0.0, MASKED).astype(np.float32)))
    yp = _out_proj(attn.reshape(b * s, -1), wo, xp.reshape(b * s, d), 512).reshape(b, s, d)
    rows_p = (ckv, kpet.transpose(0, 2, 1))
    tabs = tuple(jnp.broadcast_to(t, (db, LANE)) for t in _rope_tables(past + np.arange(1)))
    q, _, _, ckv, kpet = _c_proj(xs.reshape(1, db, d), g, wd, q_norm, wq_aug, kv_norm, wuk_aug, wuv_aug, tabs, db)
    q3 = q.reshape(db, C_HEADS, LANE)
    ckv, kpe = ckv.reshape(db, KV_LORA), kpet[0].T
    q_lat = _q_lat(q3[:, :, :NOPE_DIM].transpose(1, 0, 2), wkv[:, :, :NOPE_DIM].transpose(1, 0, 2)).transpose(1, 0, 2)
    o_lat = _paged_attn(page_table, slot, [q_lat, q3[:, :, NOPE_DIM:C_QK]],
                        [ckv.reshape(db, 1, KV_LORA), kpe.reshape(db, 1, ROPE_DIM)], ckv.reshape(db, 1, KV_LORA),
                        jnp.zeros((db, C_HEADS, 1), F32), None, None,
                        [pool_ckv, _feature_major_pages(pool_kpe)], [False, True], None, 'share0')
    ys = _c_post(o_lat.transpose(1, 0, 2), wkv[:, :, NOPE_DIM:].transpose(1, 0, 2), wo, xs.reshape(db, d)).reshape(db, 1, d)
    rows_s = (ckv.reshape(db, 1, KV_LORA), kpe.reshape(db, 1, ROPE_DIM))
    return yp, ys, rows_p, rows_s


def kernel(x_prompt, x_sample, cache_a_k, cache_a_v, cache_a_idx, cache_b_k, cache_b_v, cache_c_ckv, cache_c_kpe, state_conv, page_table, rel_table, norm_mix, norm_ffn, norm_final, a_w_in, a_w_o, b_w_in, b_lambda, b_subln, b_w_o, c_w_down, c_q_norm, c_w_uq, c_kv_norm, c_w_ukv, c_w_o, ffn_w_up, ffn_conv_w, ffn_conv_b, ffn_w_down):
    depth = norm_mix.shape[0]
    db = x_sample.shape[0]
    xp, xs = x_prompt, x_sample
    rows_p = {0: [], 1: [], 2: []}
    rows_s = {0: [], 1: [], 2: []}
    conv_p, conv_s = [], []
    for i in range(depth):
        kind, slot = i % N_MIXERS, i // N_MIXERS
        if kind == 0:
            yp, ys, rp, rs = _mixer_a(xp, xs, norm_mix[i], a_w_in[slot], a_w_o[slot], rel_table,
                                      cache_a_k, cache_a_v, cache_a_idx, slot, page_table)
        elif kind == 1:
            lam_init = 0.8 - 0.6 * math.exp(-0.3 * i)
            yp, ys, rp, rs = _mixer_b(xp, xs, norm_mix[i], b_w_in[slot], b_lambda[slot], b_subln[slot], b_w_o[slot],
                                      rel_table, lam_init, cache_b_k, cache_b_v, slot, page_table)
        else:
            yp, ys, rp, rs = _mixer_c(xp, xs, norm_mix[i], c_w_down[slot], c_q_norm[slot], c_w_uq[slot],
                                      c_kv_norm[slot], c_w_ukv[slot], c_w_o[slot],
                                      cache_c_ckv, cache_c_kpe, slot, page_table)
        rows_p[kind].append(rp)
        rows_s[kind].append(rs)
        xp, xs = yp, ys
        fw = _ffn_weights(ffn_w_up[i], ffn_conv_w[i], ffn_conv_b[i], ffn_w_down[i])
        last = i == depth - 1
        xp, cp = _ffn_prompt(xp, norm_ffn[i], fw, norm_final, last, 512)
        xs2, cs = _ffn_sample(xs.reshape(db, -1), norm_ffn[i], fw, norm_final, last, state_conv[i])
        xs = xs2.reshape(xs.shape)
        conv_p.append(cp)
        conv_s.append(cs)

    def stack(rows, j):
        return jnp.stack([r[j] for r in rows])

    a_p, b_p, c_p = rows_p[0], rows_p[1], rows_p[2]
    a_s, b_s, c_s = rows_s[0], rows_s[1], rows_s[2]
    return (xp, xs,
            stack(a_p, 0), stack(a_p, 1), stack(a_p, 2), stack(b_p, 0), stack(b_p, 1),
            stack(c_p, 0), stack(c_p, 1), jnp.stack(conv_p),
            stack(a_s, 0), stack(a_s, 1), stack(a_s, 2), stack(b_s, 0), stack(b_s, 1),
            stack(c_s, 0), stack(c_s, 1), jnp.stack(conv_s))
```

```python
import functools
import math

import numpy as np
import jax
import jax.numpy as jnp
from jax import lax
from jax.experimental import pallas as pl
from jax.experimental.pallas import tpu as pltpu

F32 = jnp.float32
BF16 = jnp.bfloat16
I32 = jnp.int32

N_MIXERS = 3
HEAD_DIM = 64
PAGE_SIZE = 128
A_HEADS, A_KV_HEADS = 16, 4
A_GROUP = A_HEADS // A_KV_HEADS
IDX_HEADS, IDX_DIM = 8, 64
TOPK_MAX = 256
A_SIZES = (A_HEADS * HEAD_DIM, A_KV_HEADS * HEAD_DIM, A_KV_HEADS * HEAD_DIM,
           IDX_HEADS * IDX_DIM, IDX_DIM, IDX_HEADS)
B_HEADS, B_KV_HEADS = 8, 4
B_GROUP = B_HEADS // B_KV_HEADS
B_SIZES = (B_HEADS * 2 * HEAD_DIM, B_KV_HEADS * 2 * HEAD_DIM, B_KV_HEADS * 2 * HEAD_DIM)
C_HEADS = 16
Q_LORA, KV_LORA, NOPE_DIM, ROPE_DIM, V_DIM = 384, 256, 64, 32, 64
C_QK = NOPE_DIM + ROPE_DIM
C_SCALE = C_QK ** -0.5
ROPE_THETA = 10000.0
REL_BUCKETS, REL_MAX_DIST, N_BIAS = 32, 128, 16
CONV_W = 3
EPS = 1e-6

LANE = 128
TQ = 256
KC = 256
FFN_CK = 256
MAX_PAGES_PER_STEP = 32
PAGE_VMEM_BUDGET = 20 * 1024 * 1024
IDX_PAGES_PER_STEP = 32
MASKED = -1e30
INT_MIN = -2 ** 31
VMEM_LIMIT = 56 * 1024 * 1024
COL_BIAS_HI, COL_BIAS_LO, COL_SHIFT = HEAD_DIM, HEAD_DIM + 1, HEAD_DIM + 2
SWEEP_ROWS = 2048
C_COL_SHIFT = C_QK


def _cparams(sem):
    return pltpu.CompilerParams(dimension_semantics=sem, vmem_limit_bytes=VMEM_LIMIT)


def _rms(x, g):
    return x * lax.rsqrt(jnp.mean(x * x, axis=-1, keepdims=True) + EPS) * g


def _rel_bucket_np(dist):
    n = np.maximum(np.asarray(dist, np.int64), 0)
    max_exact = REL_BUCKETS // 2
    nf = np.maximum(n, 1).astype(np.float32)
    large = max_exact + (np.log(nf / np.float32(max_exact)) / np.float32(math.log(REL_MAX_DIST / max_exact))
                         * np.float32(REL_BUCKETS - max_exact)).astype(np.int32)
    return np.where(n < max_exact, n, np.minimum(large, REL_BUCKETS - 1)).astype(np.int32)


def _bias_lookup(table, dist):
    bucket = _rel_bucket_np(dist)
    onehot = jnp.asarray(bucket.reshape(-1, 1) == np.arange(REL_BUCKETS)[None, :], F32)
    out = jnp.dot(onehot, table.astype(F32), precision=lax.Precision.HIGHEST)
    return out.reshape(bucket.shape + (table.shape[1],))


def _rep(x, n):
    return x if n == LANE else jnp.concatenate([x] * (n // LANE), axis=1)


def _norm_proj_kernel(x_ref, g_ref, *refs, n_out, scales):
    w_refs, o_refs = refs[:n_out], refs[n_out:]
    hb = _rms(x_ref[...], g_ref[...]).astype(BF16)
    for w_ref, o_ref, s in zip(w_refs, o_refs, scales):
        y = jnp.dot(hb, w_ref[...], preferred_element_type=F32)
        if s != 1.0:
            y = y * s
        o_ref[...] = y.astype(o_ref.dtype)


def _norm_proj(x, g, ws, dtypes, scales, tm):
    m, d = x.shape
    n_out = len(ws)
    return pl.pallas_call(
        functools.partial(_norm_proj_kernel, n_out=n_out, scales=tuple(scales)),
        grid=(m // tm,),
        in_specs=[pl.BlockSpec((tm, d), lambda i: (i, 0)), pl.BlockSpec((1, d), lambda i: (0, 0))]
        + [pl.BlockSpec(w.shape, lambda i: (0, 0)) for w in ws],
        out_specs=[pl.BlockSpec((tm, w.shape[1]), lambda i: (i, 0)) for w in ws],
        out_shape=[jax.ShapeDtypeStruct((m, w.shape[1]), dt) for w, dt in zip(ws, dtypes)],
        compiler_params=_cparams(("parallel",)),
        name="norm_proj",
    )(x, g.reshape(1, d), *ws)


def _out_proj_kernel(a_ref, w_ref, x_ref, o_ref):
    o_ref[...] = x_ref[...] + jnp.dot(a_ref[...].astype(BF16), w_ref[...], preferred_element_type=F32)


def _out_proj(a, w, x, tm):
    m, k = a.shape
    d = w.shape[1]
    return pl.pallas_call(
        _out_proj_kernel,
        grid=(m // tm,),
        in_specs=[pl.BlockSpec((tm, k), lambda i: (i, 0)), pl.BlockSpec((k, d), lambda i: (0, 0)),
                  pl.BlockSpec((tm, d), lambda i: (i, 0))],
        out_specs=pl.BlockSpec((tm, d), lambda i: (i, 0)),
        out_shape=jax.ShapeDtypeStruct((m, d), F32),
        compiler_params=_cparams(("parallel",)),
        name="out_proj",
    )(a, w, x)


def _b_post_kernel(o0_ref, o1_ref, lam_ref, sub_ref, w_ref, x_ref, o_ref, *, lam_init, n_heads):
    lp = lam_ref[...]
    lam = (jnp.exp(jnp.sum(lp[0:1] * lp[1:2], axis=-1, keepdims=True))
           - jnp.exp(jnp.sum(lp[2:3] * lp[3:4], axis=-1, keepdims=True)) + lam_init)
    d = o0_ref[...] - lam * o1_ref[...]
    hd = 2 * HEAD_DIM
    parts = []
    for h in range(n_heads):
        dh = d[:, h * hd:(h + 1) * hd]
        parts.append(_rms(dh, sub_ref[...]) * (1.0 - lam_init))
    dn = jnp.concatenate(parts, axis=-1).astype(BF16)
    o_ref[...] = x_ref[...] + jnp.dot(dn, w_ref[...], preferred_element_type=F32)


def _b_post(o0, o1, lam_p, subln, w_o, x, lam_init, tm):
    m, k = o0.shape
    d = w_o.shape[1]
    return pl.pallas_call(
        functools.partial(_b_post_kernel, lam_init=lam_init, n_heads=B_HEADS),
        grid=(m // tm,),
        in_specs=[pl.BlockSpec((tm, k), lambda i: (i, 0)), pl.BlockSpec((tm, k), lambda i: (i, 0)),
                  pl.BlockSpec(lam_p.shape, lambda i: (0, 0)), pl.BlockSpec((1, 2 * HEAD_DIM), lambda i: (0, 0)),
                  pl.BlockSpec((k, d), lambda i: (0, 0)), pl.BlockSpec((tm, d), lambda i: (i, 0))],
        out_specs=pl.BlockSpec((tm, d), lambda i: (i, 0)),
        out_shape=jax.ShapeDtypeStruct((m, d), F32),
        compiler_params=_cparams(("parallel",)),
        name="b_post",
    )(o0, o1, lam_p, subln.reshape(1, -1), w_o, x)


def _pad_heads(w, n_heads, dk):
    d = w.shape[0]
    return jnp.pad(w.reshape(d, n_heads, dk), ((0, 0), (0, 0), (0, LANE - dk))).reshape(d, n_heads * LANE)


def _a_proj_kernel(x_ref, g_ref, wq_ref, wk_ref, wv_ref, wqi_ref, wkw_ref, wkwt_ref,
                   qh_ref, qit_ref, kt_ref, vt_ref, kw_ref, kwt_ref):
    hb = _rms(x_ref[0], g_ref[...]).astype(BF16)
    yq = jnp.dot(hb, wq_ref[...], preferred_element_type=F32)
    for h in range(A_HEADS):
        qh_ref[0, 0, h] = yq[:, h * LANE:(h + 1) * LANE].astype(BF16)
    qit_ref[0] = lax.dot_general(wqi_ref[...], hb, _NT, preferred_element_type=F32).astype(BF16)
    kt_ref[0] = lax.dot_general(wk_ref[...], hb, _NT, preferred_element_type=F32)
    vt_ref[0] = lax.dot_general(wv_ref[...], hb, _NT, preferred_element_type=F32)
    kw_ref[0] = jnp.dot(hb, wkw_ref[...], preferred_element_type=F32)
    kwt_ref[0] = lax.dot_general(wkwt_ref[...], hb, _NT, preferred_element_type=F32)


def _a_proj(x, g, wq, wk_t, wv_t, wqi_t, wkw):
    b, s, d = x.shape
    nq = s // TQ
    kvd = A_KV_HEADS * HEAD_DIM
    qid = IDX_HEADS * IDX_DIM
    wkw_t = wkw.T
    full = lambda a: pl.BlockSpec(a.shape, lambda i, j: (0, 0))
    return pl.pallas_call(
        _a_proj_kernel,
        grid=(b, nq),
        in_specs=[pl.BlockSpec((1, TQ, d), lambda i, j: (i, j, 0)), pl.BlockSpec((1, d), lambda i, j: (0, 0)),
                  full(wq), full(wk_t), full(wv_t), full(wqi_t), full(wkw), full(wkw_t)],
        out_specs=[pl.BlockSpec((1, 1, A_HEADS, TQ, LANE), lambda i, j: (i, j, 0, 0, 0)),
                   pl.BlockSpec((1, qid, TQ), lambda i, j: (i, 0, j)),
                   pl.BlockSpec((1, kvd, TQ), lambda i, j: (i, 0, j)),
                   pl.BlockSpec((1, kvd, TQ), lambda i, j: (i, 0, j)),
                   pl.BlockSpec((1, TQ, LANE), lambda i, j: (i, j, 0)),
                   pl.BlockSpec((1, LANE, TQ), lambda i, j: (i, 0, j))],
        out_shape=[jax.ShapeDtypeStruct((b, nq, A_HEADS, TQ, LANE), BF16),
                   jax.ShapeDtypeStruct((b, qid, s), BF16),
                   jax.ShapeDtypeStruct((b, kvd, s), F32), jax.ShapeDtypeStruct((b, kvd, s), F32),
                   jax.ShapeDtypeStruct((b, s, LANE), F32), jax.ShapeDtypeStruct((b, LANE, s), F32)],
        compiler_params=_cparams(("parallel", "parallel")),
        name="a_proj",
    )(x, g.reshape(1, d), wq, wk_t, wv_t, wqi_t, wkw, wkw_t)


def _b_proj_kernel(x_ref, g_ref, wq_ref, wk_ref, wv_ref, qh_ref, kt_ref, v_ref):
    hb = _rms(x_ref[0], g_ref[...]).astype(BF16)
    yq = jnp.dot(hb, wq_ref[...], preferred_element_type=F32)
    for grp in range(B_KV_HEADS * 2):
        for gg in range(B_GROUP):
            blk = grp * B_GROUP + gg
            qh_ref[0, 0, grp, gg * TQ:(gg + 1) * TQ, :] = yq[:, blk * LANE:(blk + 1) * LANE].astype(BF16)
    kt_ref[0] = lax.dot_general(wk_ref[...], hb, _NT, preferred_element_type=F32)
    v_ref[0] = jnp.dot(hb, wv_ref[...], preferred_element_type=F32)


def _b_proj(x, g, wq, wk, wv):
    b, s, d = x.shape
    nq = s // TQ
    ngrp = B_KV_HEADS * 2
    kd = B_SIZES[1]
    full = lambda a: pl.BlockSpec(a.shape, lambda i, j: (0, 0))
    return pl.pallas_call(
        _b_proj_kernel,
        grid=(b, nq),
        in_specs=[pl.BlockSpec((1, TQ, d), lambda i, j: (i, j, 0)), pl.BlockSpec((1, d), lambda i, j: (0, 0)),
                  full(wq), full(wk), full(wv)],
        out_specs=[pl.BlockSpec((1, 1, ngrp, B_GROUP * TQ, LANE), lambda i, j: (i, j, 0, 0, 0)),
                   pl.BlockSpec((1, kd, TQ), lambda i, j: (i, 0, j)),
                   pl.BlockSpec((1, TQ, kd), lambda i, j: (i, j, 0))],
        out_shape=[jax.ShapeDtypeStruct((b, nq, ngrp, B_GROUP * TQ, LANE), BF16),
                   jax.ShapeDtypeStruct((b, kd, s), F32), jax.ShapeDtypeStruct((b, s, kd), F32)],
        compiler_params=_cparams(("parallel", "parallel")),
        name="b_proj",
    )(x, g.reshape(1, d), wq, wk, wv)


def _swap_halves(x, half):
    n = x.shape[-1]
    lane = lax.broadcasted_iota(I32, x.shape, x.ndim - 1)
    fwd = pltpu.roll(x, n - half, x.ndim - 1)
    bwd = pltpu.roll(x, half, x.ndim - 1)
    return jnp.where((lane // half) % 2 == 0, fwd, bwd)


def _c_proj_kernel(x_ref, g_ref, wd_ref, qn_ref, wuq_ref, kvn_ref, wuk_ref, wuv_ref,
                   cq_ref, sq_ref, ck_ref, sk_ref, q_o, k_o, v_o, ckv_o, kpet_o):
    hb = _rms(x_ref[0], g_ref[...]).astype(BF16)
    down = jnp.dot(hb, wd_ref[...], preferred_element_type=F32)
    cq = down[:, :Q_LORA]
    ckv = down[:, Q_LORA:Q_LORA + KV_LORA]
    kp = down[:, Q_LORA + KV_LORA:]
    qf = jnp.dot(_rms(cq, qn_ref[...]).astype(BF16), wuq_ref[...], preferred_element_type=F32)
    half = ROPE_DIM // 2
    cq_t, sq_t = cq_ref[...], sq_ref[...]
    for h in range(C_HEADS):
        x = qf[:, h * LANE:(h + 1) * LANE]
        x = x * cq_t + _swap_halves(x, half) * sq_t
        q_o[0, :, h * LANE:(h + 1) * LANE] = (x * C_SCALE).astype(BF16)
    ckv_n = _rms(ckv, kvn_ref[...])
    ckv_o[0] = ckv_n
    kp = kp * ck_ref[...] + _swap_halves(kp, half) * sk_ref[...]
    kpet_o[0] = kp.T[:ROPE_DIM]
    lane = lax.broadcasted_iota(I32, kp.shape, 1)
    k_tail = pltpu.roll(kp, NOPE_DIM, 1) + jnp.where(lane == C_COL_SHIFT, 1.0, 0.0)
    v_tail = jnp.where(lane == V_DIM, 1.0, 0.0)
    cb = ckv_n.astype(BF16)
    kn = jnp.dot(cb, wuk_ref[...], preferred_element_type=F32)
    vv = jnp.dot(cb, wuv_ref[...], preferred_element_type=F32)
    for h in range(C_HEADS):
        k_o[0, :, h * LANE:(h + 1) * LANE] = (kn[:, h * LANE:(h + 1) * LANE] + k_tail).astype(BF16)
        v_o[0, :, h * LANE:(h + 1) * LANE] = (vv[:, h * LANE:(h + 1) * LANE] + v_tail).astype(BF16)


def _c_proj(x, g, w_down, q_norm, w_uq_aug, kv_norm, w_uk_aug, w_uv_aug, tabs, tm):
    b, s, d = x.shape
    hw = C_HEADS * LANE
    full = lambda a: pl.BlockSpec(a.shape, lambda i, j: (0, 0))
    rows = lambda n: pl.BlockSpec((1, tm, n), lambda i, j: (i, j, 0))
    pos = pl.BlockSpec((tm, LANE), lambda i, j: (j, 0))
    qn2, kvn2 = q_norm.reshape(1, -1), kv_norm.reshape(1, -1)
    return pl.pallas_call(
        _c_proj_kernel,
        grid=(b, s // tm),
        in_specs=[rows(d), pl.BlockSpec((1, d), lambda i, j: (0, 0)), full(w_down), full(qn2), full(w_uq_aug),
                  full(kvn2), full(w_uk_aug), full(w_uv_aug), pos, pos, pos, pos],
        out_specs=[rows(hw), rows(hw), rows(hw), rows(KV_LORA),
                   pl.BlockSpec((1, ROPE_DIM, tm), lambda i, j: (i, 0, j))],
        out_shape=[jax.ShapeDtypeStruct((b, s, hw), BF16)] * 3
        + [jax.ShapeDtypeStruct((b, s, KV_LORA), F32), jax.ShapeDtypeStruct((b, ROPE_DIM, s), F32)],
        compiler_params=_cparams(("parallel", "parallel")),
        name="c_proj",
    )(x, g.reshape(1, d), w_down, qn2, w_uq_aug, kvn2, w_uk_aug, w_uv_aug, *tabs)


def _rope_tables(positions):
    half = ROPE_DIM // 2
    inv = ROPE_THETA ** (-np.arange(half, dtype=np.float32) / np.float32(half))
    ang = np.asarray(positions, np.float32)[:, None] * inv[None, :]
    c, s = np.cos(ang).astype(np.float32), np.sin(ang).astype(np.float32)
    n = ang.shape[0]

    def place(first):
        cos = np.ones((n, LANE), np.float32)
        sin = np.zeros((n, LANE), np.float32)
        cos[:, first:first + ROPE_DIM] = np.concatenate([c, c], axis=-1)
        sin[:, first:first + ROPE_DIM] = np.concatenate([-s, s], axis=-1)
        return jnp.asarray(cos), jnp.asarray(sin)

    return place(NOPE_DIM) + place(0)


def _silu(x):
    return x * (1.0 / (1.0 + jnp.exp(-x)))


def _ffn_prompt_kernel(x_ref, g_ref, wg_ref, wv_ref, cw_ref, wd_ref, gf_ref,
                       y_ref, st_ref, carry_ref, acc_ref, *, nchunk, final_norm):
    t = pl.program_id(1)
    nt = pl.num_programs(1)

    @pl.when(t == 0)
    def _():
        carry_ref[...] = jnp.zeros_like(carry_ref)

    x = x_ref[0]
    tm = x.shape[0]
    hb = _rms(x, g_ref[...]).astype(BF16)
    acc_ref[...] = x
    row = lax.broadcasted_iota(I32, (tm, FFN_CK), 0)

    def conv_part(c, part, w_ref):
        u = jnp.dot(hb, w_ref[c], preferred_element_type=F32)
        idx = part * nchunk + c
        prev = carry_ref[idx]
        p1, p2 = prev[7:8], prev[6:7]
        u1 = jnp.where(row == 0, p1, pltpu.roll(u, 1, 0))
        u2 = jnp.where(row == 0, p2, jnp.where(row == 1, p1, pltpu.roll(u, 2, 0)))
        cw = cw_ref[idx]
        carry_ref[idx, 6:8, :] = u[tm - 2:tm]
        return cw[3:4] + cw[2:3] * u + cw[1:2] * u1 + cw[0:1] * u2

    def body(c, _):
        gate = conv_part(c, 0, wg_ref)
        val = conv_part(c, 1, wv_ref)
        act = (_silu(gate) * val).astype(BF16)
        acc_ref[...] += jnp.dot(act, wd_ref[c], preferred_element_type=F32)
        return 0

    lax.fori_loop(0, nchunk, body, 0, unroll=True)
    y = acc_ref[...]
    if final_norm:
        y = _rms(y, gf_ref[...])
    y_ref[0] = y

    @pl.when(t == nt - 1)
    def _():
        st_ref[0] = carry_ref[...]


def _ffn_weights(w_up, conv_w, conv_b, w_down):
    d, two_f = w_up.shape
    f = two_f // 2
    nc = f // FFN_CK
    wu = w_up.astype(BF16)
    wg = wu[:, :f].reshape(d, nc, FFN_CK).transpose(1, 0, 2)
    wv = wu[:, f:].reshape(d, nc, FFN_CK).transpose(1, 0, 2)
    wd = w_down.astype(BF16).reshape(nc, FFN_CK, d)
    cw = jnp.concatenate([conv_w, conv_b[None], jnp.zeros((8 - CONV_W - 1, two_f), F32)], axis=0)
    cw = cw.reshape(8, 2 * nc, FFN_CK).transpose(1, 0, 2)
    return wg, wv, wd, cw, nc


def _ffn_prompt(x, g, fw, g_final, final_norm, tm):
    b, s, d = x.shape
    wg, wv, wd, cw, nc = fw
    y, st = pl.pallas_call(
        functools.partial(_ffn_prompt_kernel, nchunk=nc, final_norm=final_norm),
        grid=(b, s // tm),
        in_specs=[pl.BlockSpec((1, tm, d), lambda i, t: (i, t, 0)), pl.BlockSpec((1, d), lambda i, t: (0, 0)),
                  pl.BlockSpec(wg.shape, lambda i, t: (0, 0, 0)), pl.BlockSpec(wv.shape, lambda i, t: (0, 0, 0)),
                  pl.BlockSpec(cw.shape, lambda i, t: (0, 0, 0)),
                  pl.BlockSpec(wd.shape, lambda i, t: (0, 0, 0)), pl.BlockSpec((1, d), lambda i, t: (0, 0))],
        out_specs=[pl.BlockSpec((1, tm, d), lambda i, t: (i, t, 0)),
                   pl.BlockSpec((1, 2 * nc, 8, FFN_CK), lambda i, t: (i, 0, 0, 0))],
        out_shape=[jax.ShapeDtypeStruct((b, s, d), F32), jax.ShapeDtypeStruct((b, 2 * nc, 8, FFN_CK), F32)],
        scratch_shapes=[pltpu.VMEM((2 * nc, 8, FFN_CK), F32), pltpu.VMEM((tm, d), F32)],
        compiler_params=_cparams(("parallel", "arbitrary")),
        name="ffn_prompt",
    )(x, g.reshape(1, d), wg, wv, cw, wd, g_final.reshape(1, d))
    st = st[:, :, 6:8, :].transpose(0, 2, 1, 3).reshape(b, 2, 2 * nc * FFN_CK)
    return y, st


def _ffn_sample_kernel(x_ref, g_ref, wg_ref, wv_ref, cw_ref, wd_ref, gf_ref, p0_ref, p1_ref,
                       y_ref, u_ref, acc_ref, *, nchunk, final_norm):
    x = x_ref[...]
    hb = _rms(x, g_ref[...]).astype(BF16)
    acc_ref[...] = x

    def conv_part(c, part, w_ref):
        u = jnp.dot(hb, w_ref[c], preferred_element_type=F32)
        idx = part * nchunk + c
        u_ref[idx] = u
        cw = cw_ref[idx]
        return cw[3:4] + cw[2:3] * u + cw[1:2] * p1_ref[idx] + cw[0:1] * p0_ref[idx]

    def body(c, _):
        gate = conv_part(c, 0, wg_ref)
        val = conv_part(c, 1, wv_ref)
        act = (_silu(gate) * val).astype(BF16)
        acc_ref[...] += jnp.dot(act, wd_ref[c], preferred_element_type=F32)
        return 0

    lax.fori_loop(0, nchunk, body, 0)
    y = acc_ref[...]
    if final_norm:
        y = _rms(y, gf_ref[...])
    y_ref[...] = y


def _ffn_sample(x, g, fw, g_final, final_norm, state):
    m, d = x.shape
    wg, wv, wd, cw, nc = fw
    st = state.reshape(m, 2, 2 * nc, FFN_CK).transpose(1, 2, 0, 3)
    full3 = lambda a: pl.BlockSpec(a.shape, lambda i: (0, 0, 0))
    full2 = lambda a: pl.BlockSpec(a.shape, lambda i: (0, 0))
    p0, p1 = st[0], st[1]
    y, u = pl.pallas_call(
        functools.partial(_ffn_sample_kernel, nchunk=nc, final_norm=final_norm),
        grid=(1,),
        in_specs=[full2(x), pl.BlockSpec((1, d), lambda i: (0, 0)), full3(wg), full3(wv), full3(cw), full3(wd),
                  pl.BlockSpec((1, d), lambda i: (0, 0)), full3(p0), full3(p1)],
        out_specs=[full2(x), full3(p0)],
        out_shape=[jax.ShapeDtypeStruct((m, d), F32), jax.ShapeDtypeStruct(p0.shape, F32)],
        scratch_shapes=[pltpu.VMEM((m, d), F32)],
        compiler_params=_cparams(("arbitrary",)),
        name="ffn_sample",
    )(x, g.reshape(1, d), wg, wv, cw, wd, g_final.reshape(1, d), p0, p1)
    u = u.transpose(1, 0, 2).reshape(m, 2 * nc * FFN_CK)
    return y, jnp.stack([state[:, 1], u], axis=1)


def _sortable(s):
    b = lax.bitcast_convert_type(s + 0.0, I32)
    return jnp.where(b < 0, b ^ 0x7FFFFFFF, b)


_KEY_NEG_INF = int(np.int32(np.array(-np.inf, np.float32).view(np.int32)) ^ np.int32(0x7FFFFFFF))


def _lane_total(part):
    return jnp.dot(part.astype(BF16), jnp.ones((LANE, LANE), BF16), preferred_element_type=F32)


def _count(key_ref, nchunks, rows, pred):
    kc = key_ref.shape[2]

    def body(c, acc):
        hit = jnp.where(pred(key_ref[c]), 1.0, 0.0)
        for t in range(kc // LANE):
            acc = acc + hit[:, t * LANE:(t + 1) * LANE]
        return acc

    return _lane_total(lax.fori_loop(0, nchunks, body, jnp.zeros((rows, LANE), F32)))


def _kth_largest(key_ref, nchunks, rows, k):
    kc = key_ref.shape[2]

    def bit_step(i, carry):
        thr, cnt_thr = carry
        cand = thr + lax.shift_left(jnp.int32(1), 31 - i)
        cand_w = _rep(cand, kc)
        cnt = _count(key_ref, nchunks, rows, lambda key: key >= cand_w)
        ok = cnt >= k
        return jnp.where(ok, cand, thr), jnp.where(ok, cnt, cnt_thr)

    total = (nchunks * kc).astype(F32) if not isinstance(nchunks, int) else float(nchunks * kc)
    init = (jnp.full((rows, LANE), INT_MIN, I32), jnp.zeros((rows, LANE), F32) + total)
    return lax.fori_loop(0, 32, bit_step, init)


def _write_select_mask(key_ref, ma_ref, nchunks, rows, k, thr, cnt_thr, tri_ref):
    kc = key_ref.shape[2]
    thr_w = _rep(thr, kc)
    need_ties = jnp.max(jnp.where((cnt_thr > k) & (thr > _KEY_NEG_INF), 1.0, 0.0)) > 0.0

    @pl.when(jnp.logical_not(need_ties))
    def _():
        def body(c, _):
            ma_ref[c] = jnp.where(key_ref[c] >= thr_w, 0.0, MASKED)
            return 0
        lax.fori_loop(0, nchunks, body, 0)

    @pl.when(need_ties)
    def _():
        n_gt = _count(key_ref, nchunks, rows, lambda key: key > thr_w)
        room_w = _rep(k - n_gt, kc)
        ones = jnp.ones((kc, LANE), BF16)

        def body(c, seen):
            key = key_ref[c]
            eq = key == thr_w
            eqb = jnp.where(eq, 1.0, 0.0).astype(BF16)
            prefix = _rep(seen, kc) + jnp.dot(eqb, tri_ref[...], preferred_element_type=F32)
            keep = (key > thr_w) | (eq & (prefix <= room_w))
            ma_ref[c] = jnp.where(keep, 0.0, MASKED)
            return seen + jnp.dot(eqb, ones, preferred_element_type=F32)
        lax.fori_loop(0, nchunks, body, jnp.zeros((rows, LANE), F32))


def _tri(n):
    return jnp.asarray(np.triu(np.ones((n, n), np.float32)), BF16)


def _count_t(key_ref, nchunks, nq, pred):
    kc = key_ref.shape[1]

    def body(c, acc):
        hit = jnp.where(pred(key_ref[c]), 1.0, 0.0)
        return acc + jnp.sum(hit.reshape(kc // 8, 8, nq), axis=0)

    acc = lax.fori_loop(0, nchunks, body, jnp.zeros((8, nq), F32))
    return jnp.sum(acc, axis=0, keepdims=True)


def _kth_largest_t(key_ref, nchunks, nq, k):
    kc = key_ref.shape[1]

    def bit_step(i, carry):
        thr, cnt_thr = carry
        cand = thr + lax.shift_left(jnp.int32(1), 31 - i)
        cnt = _count_t(key_ref, nchunks, nq, lambda key: key >= cand)
        ok = cnt >= k
        return jnp.where(ok, cand, thr), jnp.where(ok, cnt, cnt_thr)

    init = (jnp.full((1, nq), INT_MIN, I32), jnp.zeros((1, nq), F32) + (nchunks * kc).astype(F32))
    return lax.fori_loop(0, 32, bit_step, init)


def _write_select_mask_t(key_ref, ma_ref, nchunks, nq, k, thr, cnt_thr, tril_ref):
    kc = key_ref.shape[1]
    need_ties = jnp.max(jnp.where((cnt_thr > k) & (thr > _KEY_NEG_INF), 1.0, 0.0)) > 0.0

    @pl.when(jnp.logical_not(need_ties))
    def _():
        def body(c, _):
            ma_ref[c] = jnp.where(key_ref[c] >= thr, 0.0, MASKED).T
            return 0
        lax.fori_loop(0, nchunks, body, 0)

    @pl.when(need_ties)
    def _():
        room = k - _count_t(key_ref, nchunks, nq, lambda key: key > thr)

        def body(c, seen):
            key = key_ref[c]
            eq = key == thr
            eqf = jnp.where(eq, 1.0, 0.0)
            prefix = seen + jnp.dot(tril_ref[...], eqf.astype(BF16), preferred_element_type=F32)
            keep = (key > thr) | (eq & (prefix <= room))
            ma_ref[c] = jnp.where(keep, 0.0, MASKED).T
            return seen + jnp.sum(jnp.sum(eqf.reshape(kc // 8, 8, nq), axis=0), axis=0, keepdims=True)
        lax.fori_loop(0, nchunks, body, jnp.zeros((1, nq), F32))


def _tril(n):
    return jnp.asarray(np.tril(np.ones((n, n), np.float32)), BF16)


def _ones_rows(n_ones, shape):
    return jnp.where(lax.broadcasted_iota(I32, shape, 0) < n_ones, 1.0, 0.0).astype(BF16)


def _ones_cols(n_ones, shape):
    return jnp.where(lax.broadcasted_iota(I32, shape, 1) < n_ones, 1.0, 0.0).astype(BF16)


def _causal_two_pass(qb, n_groups, rows, qa_ref, m_ref, acc_ref, logits_at, pv_at, tile_at, add_at, col_shift,
                     s_ref=None):
    n_stack = rows // TQ
    lane = lax.broadcasted_iota(I32, (rows, LANE), 1)

    def logits(g, c, off, w):
        s = logits_at(g, off, qa_ref[g])
        if add_at is not None:
            s = (s.reshape(n_stack, TQ, KC) + add_at(c)[None]).reshape(rows, KC)
        tile = tile_at(g, w) if w is not None else None
        return s if tile is None else s + tile

    def sweep(step):
        def far(c, _):
            step(c, None)
            return 0
        lax.fori_loop(0, jnp.maximum(qb - 1, 0), far, 0)

        @pl.when(qb >= 1)
        def _():
            step(qb - 1, 0)
        step(qb, 1)

    m_ref[...] = jnp.full(m_ref.shape, -jnp.inf, F32)

    if s_ref is not None:
        acc_ref[...] = jnp.zeros(acc_ref.shape, F32)
        batch = s_ref.shape[0]
        for g0 in range(0, n_groups, batch):
            grp = list(range(g0, min(g0 + batch, n_groups)))

            def keep_max(c, w, grp=grp):
                off = pl.multiple_of(c * KC, KC)
                for gi, g in enumerate(grp):
                    s = logits(g, c, off, w)
                    s_ref[gi, c] = s
                    mx = s[:, :LANE]
                    for t in range(1, KC // LANE):
                        mx = jnp.maximum(mx, s[:, t * LANE:(t + 1) * LANE])
                    m_ref[g] = jnp.maximum(m_ref[g], mx)

            sweep(keep_max)
            for g in grp:
                m_ref[g] = jnp.broadcast_to(jnp.max(m_ref[g], axis=-1, keepdims=True), (rows, LANE))

            def accumulate(c, w, grp=grp):
                off = pl.multiple_of(c * KC, KC)
                for gi, g in enumerate(grp):
                    p = jnp.exp(s_ref[gi, c] - _rep(m_ref[g], KC)).astype(BF16)
                    acc_ref[g] += pv_at(g, off, p)

            sweep(accumulate)
        return

    def step_max(c, w):
        off = pl.multiple_of(c * KC, KC)
        for g in range(n_groups):
            s = logits(g, c, off, w)
            mx = s[:, :LANE]
            for t in range(1, KC // LANE):
                mx = jnp.maximum(mx, s[:, t * LANE:(t + 1) * LANE])
            m_ref[g] = jnp.maximum(m_ref[g], mx)

    sweep(step_max)
    for g in range(n_groups):
        m = jnp.max(m_ref[g], axis=-1, keepdims=True)
        qa_ref[g] = jnp.where(lane == col_shift, jnp.broadcast_to(-m, (rows, LANE)).astype(BF16), qa_ref[g])
    acc_ref[...] = jnp.zeros(acc_ref.shape, F32)

    def step_acc(c, w):
        off = pl.multiple_of(c * KC, KC)
        for g in range(n_groups):
            p = jnp.exp(logits(g, c, off, w)).astype(BF16)
            acc_ref[g] += pv_at(g, off, p)

    sweep(step_acc)


def _far_cols(far_ref, first, n, lane):
    hi = jnp.concatenate([jnp.full((TQ, LANE), far_ref[2 * (first + i)], F32) for i in range(n)], axis=0)
    lo = jnp.concatenate([jnp.full((TQ, LANE), far_ref[2 * (first + i) + 1], F32) for i in range(n)], axis=0)
    return jnp.where(lane == COL_BIAS_HI, hi, jnp.where(lane == COL_BIAS_LO, lo, 0.0))


_NT = (((1,), (1,)), ((), ()))


def _a_prompt_kernel(far_ref, qit_ref, kwt_ref, kw_ref, q_ref, kt_ref, vt_ref, tile_ref, tril_ref,
                     o_ref, key_ref, ma_ref, qip_ref, qa_ref, m_ref, acc_ref, s_ref, *, topk):
    qb = pl.program_id(1)
    rows = A_GROUP * TQ
    zpad = jnp.zeros((LANE - IDX_DIM, TQ), BF16)
    for h in range(IDX_HEADS):
        qip_ref[h] = jnp.concatenate([qit_ref[0, h * IDX_DIM:(h + 1) * IDX_DIM, :], zpad], axis=0)
    wt = kwt_ref[0, IDX_DIM:IDX_DIM + IDX_HEADS, :] * (IDX_HEADS ** -0.5)

    def score_t(c):
        off = pl.multiple_of(c * KC, KC)
        kb = kw_ref[0, pl.ds(off, KC), :].astype(BF16)
        s = None
        for h in range(IDX_HEADS):
            d = jnp.maximum(jnp.dot(kb, qip_ref[h], preferred_element_type=F32), 0.0) * wt[h:h + 1]
            s = d if s is None else s + d
        return s

    def score_body(c, _):
        key_ref[c] = _sortable(score_t(c))
        return 0

    lax.fori_loop(0, qb, score_body, 0)
    causal = (lax.broadcasted_iota(I32, (KC, TQ), 0) <= lax.broadcasted_iota(I32, (KC, TQ), 1))
    key_ref[qb] = _sortable(jnp.where(causal, score_t(qb), -jnp.inf))
    thr, cnt_thr = _kth_largest_t(key_ref, qb + 1, TQ, topk)
    _write_select_mask_t(key_ref, ma_ref, qb + 1, TQ, topk, thr, cnt_thr, tril_ref)
    lane = lax.broadcasted_iota(I32, (rows, LANE), 1)
    for kv in range(A_KV_HEADS):
        q = q_ref[0, 0, kv * A_GROUP:(kv + 1) * A_GROUP].reshape(rows, LANE)
        qa_ref[kv] = jnp.where(lane < HEAD_DIM, q, _far_cols(far_ref, kv * A_GROUP, A_GROUP, lane).astype(BF16))
    k_ones = _ones_rows(3, (LANE - HEAD_DIM, KC))
    v_ones = _ones_rows(1, (LANE - HEAD_DIM, KC))

    def logits_at(kv, off, qa):
        kt = jnp.concatenate([kt_ref[0, kv * HEAD_DIM:(kv + 1) * HEAD_DIM, pl.ds(off, KC)].astype(BF16), k_ones], axis=0)
        return jnp.dot(qa, kt, preferred_element_type=F32)

    def pv_at(kv, off, p):
        vt = jnp.concatenate([vt_ref[0, kv * HEAD_DIM:(kv + 1) * HEAD_DIM, pl.ds(off, KC)].astype(BF16), v_ones], axis=0)
        return lax.dot_general(p, vt, _NT, preferred_element_type=F32)

    _causal_two_pass(qb, A_KV_HEADS, rows, qa_ref, m_ref, acc_ref, logits_at, pv_at,
                     lambda kv, w: tile_ref[kv, w], lambda c: ma_ref[c], COL_SHIFT, s_ref)
    for kv in range(A_KV_HEADS):
        acc = acc_ref[kv]
        o = acc[:, :HEAD_DIM] / acc[:, HEAD_DIM:HEAD_DIM + 1]
        for g in range(A_GROUP):
            h = kv * A_GROUP + g
            o_ref[0, :, h * HEAD_DIM:(h + 1) * HEAD_DIM] = o[g * TQ:(g + 1) * TQ].astype(o_ref.dtype)


def _a_prompt_attn(far, qit, kwt, kw, qh, kt, vt, tiles, topk):
    b, nq = qh.shape[0], qh.shape[1]
    s = nq * TQ
    nc = s // KC
    rows = A_GROUP * TQ
    kvd = A_KV_HEADS * HEAD_DIM
    grid_spec = pltpu.PrefetchScalarGridSpec(
        num_scalar_prefetch=1,
        grid=(b, nq),
        in_specs=[
            pl.BlockSpec((1, IDX_HEADS * IDX_DIM, TQ), lambda i, j, f: (i, 0, j)),
            pl.BlockSpec((1, LANE, TQ), lambda i, j, f: (i, 0, j)),
            pl.BlockSpec((1, s, LANE), lambda i, j, f: (i, 0, 0)),
            pl.BlockSpec((1, 1, A_HEADS, TQ, LANE), lambda i, j, f: (i, j, 0, 0, 0)),
            pl.BlockSpec((1, kvd, s), lambda i, j, f: (i, 0, 0)),
            pl.BlockSpec((1, kvd, s), lambda i, j, f: (i, 0, 0)),
            pl.BlockSpec(tiles.shape, lambda i, j, f: (0, 0, 0, 0)),
            pl.BlockSpec((KC, KC), lambda i, j, f: (0, 0)),
        ],
        out_specs=pl.BlockSpec((1, TQ, A_HEADS * HEAD_DIM), lambda i, j, f: (i, j, 0)),
        scratch_shapes=[pltpu.VMEM((nc, KC, TQ), I32), pltpu.VMEM((nc, TQ, KC), F32),
                        pltpu.VMEM((IDX_HEADS, LANE, TQ), BF16), pltpu.VMEM((A_KV_HEADS, rows, LANE), BF16),
                        pltpu.VMEM((A_KV_HEADS, rows, LANE), F32), pltpu.VMEM((A_KV_HEADS, rows, LANE), F32),
                        pltpu.VMEM((SWEEP_ROWS // rows, nc, rows, KC), F32)],
    )
    return pl.pallas_call(
        functools.partial(_a_prompt_kernel, topk=topk),
        grid_spec=grid_spec,
        out_shape=jax.ShapeDtypeStruct((b, s, A_HEADS * HEAD_DIM), BF16),
        compiler_params=_cparams(("parallel", "arbitrary")),
        name="a_prompt_attn",
    )(far, qit, kwt, kw, qh, kt, vt, tiles, _tril(KC))


def _b_prompt_kernel(far_ref, q_ref, kt_ref, v_ref, tile_ref, o0_ref, o1_ref, qa_ref, m_ref, acc_ref):
    qb = pl.program_id(1)
    hd = 2 * HEAD_DIM
    rows = B_GROUP * TQ
    ngrp = B_KV_HEADS * 2
    lane = lax.broadcasted_iota(I32, (rows, LANE), 1)
    for grp in range(ngrp):
        qa_ref[grp] = jnp.where(lane < HEAD_DIM, q_ref[0, 0, grp],
                                _far_cols(far_ref, grp * B_GROUP, B_GROUP, lane).astype(BF16))
    k_ones = _ones_rows(3, (LANE - HEAD_DIM, KC))
    v_ones = _ones_cols(1, (KC, LANE))

    def logits_at(grp, off, qa):
        kt = jnp.concatenate([kt_ref[0, grp * HEAD_DIM:(grp + 1) * HEAD_DIM, pl.ds(off, KC)].astype(BF16), k_ones], axis=0)
        return jnp.dot(qa, kt, preferred_element_type=F32)

    def pv_at(grp, off, p):
        kv = grp // 2
        v = jnp.concatenate([v_ref[0, pl.ds(off, KC), kv * hd:(kv + 1) * hd].astype(BF16), v_ones], axis=1)
        return jnp.dot(p, v, preferred_element_type=F32)

    _causal_two_pass(qb, ngrp, rows, qa_ref, m_ref, acc_ref, logits_at, pv_at,
                     lambda grp, w: tile_ref[grp, w], None, COL_SHIFT)
    for grp in range(ngrp):
        kv, j = grp // 2, grp % 2
        acc = acc_ref[grp]
        o = acc[:, :hd] / acc[:, hd:hd + 1]
        o_ref = o0_ref if j == 0 else o1_ref
        for g in range(B_GROUP):
            h = kv * B_GROUP + g
            o_ref[0, :, h * hd:(h + 1) * hd] = o[g * TQ:(g + 1) * TQ]


def _b_prompt_attn(far, qh, kt, v, tiles):
    b, nq = qh.shape[0], qh.shape[1]
    s = nq * TQ
    hd = 2 * HEAD_DIM
    ngrp = B_KV_HEADS * 2
    rows = B_GROUP * TQ
    grid_spec = pltpu.PrefetchScalarGridSpec(
        num_scalar_prefetch=1,
        grid=(b, nq),
        in_specs=[
            pl.BlockSpec((1, 1, ngrp, rows, LANE), lambda i, j, f: (i, j, 0, 0, 0)),
            pl.BlockSpec((1, ngrp * HEAD_DIM, s), lambda i, j, f: (i, 0, 0)),
            pl.BlockSpec((1, s, B_KV_HEADS * hd), lambda i, j, f: (i, 0, 0)),
            pl.BlockSpec(tiles.shape, lambda i, j, f: (0, 0, 0, 0)),
        ],
        out_specs=[pl.BlockSpec((1, TQ, B_HEADS * hd), lambda i, j, f: (i, j, 0))] * 2,
        scratch_shapes=[pltpu.VMEM((ngrp, rows, LANE), BF16), pltpu.VMEM((ngrp, rows, LANE), F32),
                        pltpu.VMEM((ngrp, rows, 2 * LANE), F32)],
    )
    return pl.pallas_call(
        _b_prompt_kernel,
        grid_spec=grid_spec,
        out_shape=[jax.ShapeDtypeStruct((b, s, B_HEADS * hd), F32)] * 2,
        compiler_params=_cparams(("parallel", "arbitrary")),
        name="b_prompt_attn",
    )(far, qh, kt, v, tiles)


def _c_prompt_kernel(q_ref, k_ref, v_ref, tile_ref, o_ref, qa_ref, m_ref, acc_ref):
    qb = pl.program_id(1)
    for h in range(C_HEADS):
        qa_ref[h] = q_ref[0, :, h * LANE:(h + 1) * LANE]

    def logits_at(h, off, qa):
        return lax.dot_general(qa, k_ref[0, pl.ds(off, KC), h * LANE:(h + 1) * LANE], _NT, preferred_element_type=F32)

    def pv_at(h, off, p):
        return jnp.dot(p, v_ref[0, pl.ds(off, KC), h * LANE:(h + 1) * LANE], preferred_element_type=F32)

    _causal_two_pass(qb, C_HEADS, TQ, qa_ref, m_ref, acc_ref, logits_at, pv_at,
                     lambda h, w: tile_ref[...] if w == 1 else None, None, C_COL_SHIFT)
    for h in range(C_HEADS):
        acc = acc_ref[h]
        o_ref[0, :, h * V_DIM:(h + 1) * V_DIM] = (acc[:, :V_DIM] / acc[:, V_DIM:V_DIM + 1]).astype(o_ref.dtype)


def _c_prompt_attn(q, k, v, tile):
    b, s, hw = q.shape
    nq = s // TQ
    return pl.pallas_call(
        _c_prompt_kernel,
        grid=(b, nq),
        in_specs=[pl.BlockSpec((1, TQ, hw), lambda i, j: (i, j, 0)),
                  pl.BlockSpec((1, s, hw), lambda i, j: (i, 0, 0)),
                  pl.BlockSpec((1, s, hw), lambda i, j: (i, 0, 0)),
                  pl.BlockSpec((TQ, KC), lambda i, j: (0, 0))],
        out_specs=pl.BlockSpec((1, TQ, C_HEADS * V_DIM), lambda i, j: (i, j, 0)),
        out_shape=jax.ShapeDtypeStruct((b, s, C_HEADS * V_DIM), BF16),
        scratch_shapes=[pltpu.VMEM((C_HEADS, TQ, LANE), BF16), pltpu.VMEM((C_HEADS, TQ, LANE), F32),
                        pltpu.VMEM((C_HEADS, TQ, LANE), F32)],
        compiler_params=_cparams(("parallel", "arbitrary")),
        name="c_prompt_attn",
    )(q, k, v, tile)


def _bias_tiles(rel_table, cols):
    i = np.arange(TQ)[:, None]
    j = np.arange(KC)[None, :]
    dist = np.stack([i - j + KC, i - j])
    assert np.all(_rel_bucket_np(np.arange(KC + 1, 16 * KC)) == REL_BUCKETS - 1)
    tab = rel_table.astype(F32)[:, np.asarray(cols)]
    far = tab[REL_BUCKETS - 1]
    hi = far.astype(BF16).astype(F32)
    lo = (far - hi).astype(BF16).astype(F32)
    tiles = jnp.transpose(_bias_lookup(tab, dist), (3, 0, 1, 2)) - (hi + lo)[:, None, None, None]
    tiles = jnp.where(jnp.asarray((dist >= 0)[None]), tiles, MASKED)
    return tiles, jnp.stack([hi, lo], axis=1).reshape(-1)


def _page_specs(block_tail, slot, n_pages, pps):
    specs = []
    for p in range(pps):
        def imap(i, g, pt, p=p):
            return (slot, pt[i * n_pages + g * pps + p]) + (0,) * len(block_tail)
        specs.append(pl.BlockSpec((None, None) + block_tail, imap))
    return specs


def _a_idx_kernel(pt_ref, qi_ref, wi_ref, *refs, pps):
    page_refs, o_ref = refs[:pps], refs[pps]
    qi = qi_ref[0]
    wi = wi_ref[0] * (IDX_HEADS ** -0.5)
    outs = []
    for p in range(pps):
        d = jnp.dot(qi, page_refs[p][...].astype(BF16), preferred_element_type=F32)
        outs.append(jnp.sum(jnp.maximum(d, 0.0) * wi, axis=0, keepdims=True))
    o_ref[0] = jnp.concatenate(outs, axis=-1)


def _a_idx_scores(page_table, qi, wi, pool_idx_t, slot):
    db = qi.shape[0]
    n_pages = page_table.shape[1]
    pps = IDX_PAGES_PER_STEP
    grid_spec = pltpu.PrefetchScalarGridSpec(
        num_scalar_prefetch=1,
        grid=(db, n_pages // pps),
        in_specs=[pl.BlockSpec((1, 2 * IDX_HEADS, IDX_DIM), lambda i, g, pt: (i, 0, 0)),
                  pl.BlockSpec((1, 2 * IDX_HEADS, 1), lambda i, g, pt: (i, 0, 0))]
        + _page_specs((IDX_DIM, PAGE_SIZE), slot, n_pages, pps),
        out_specs=pl.BlockSpec((1, 1, pps * PAGE_SIZE), lambda i, g, pt: (i, 0, g)),
    )
    return pl.pallas_call(
        functools.partial(_a_idx_kernel, pps=pps),
        grid_spec=grid_spec,
        out_shape=jax.ShapeDtypeStruct((db, 1, n_pages * PAGE_SIZE), F32),
        compiler_params=_cparams(("parallel", "arbitrary")),
        name="a_idx_scores",
    )(page_table.reshape(-1), qi, wi, *([pool_idx_t] * pps))


def _a_select_kernel(sp_ref, qi_ref, ki_ref, wi_ref, tri_ref, ma_ref, key_ref, *, topk, n_past):
    rows = sp_ref.shape[0]
    nch = n_past // KC
    for c in range(nch):
        key_ref[c] = _sortable(sp_ref[:, c * KC:(c + 1) * KC])
    d = jnp.sum(qi_ref[...] * ki_ref[...], axis=-1)
    s_new = jnp.sum(jnp.maximum(d, 0.0) * (wi_ref[...] * (IDX_HEADS ** -0.5)), axis=-1, keepdims=True)
    lane = lax.broadcasted_iota(I32, (rows, KC), 1)
    key_ref[nch] = _sortable(jnp.where(lane == 0, s_new, -jnp.inf))
    thr, cnt_thr = _kth_largest(key_ref, nch + 1, rows, topk)
    _write_select_mask(key_ref, ma_ref, nch + 1, rows, topk, thr, cnt_thr, tri_ref)


def _a_select(scores_past, qi, ki, wi, topk):
    db, n_past = scores_past.shape
    nch = n_past // KC
    args = (scores_past, qi, ki, wi, _tri(KC))
    return pl.pallas_call(
        functools.partial(_a_select_kernel, topk=topk, n_past=n_past),
        grid=(1,),
        in_specs=[pl.BlockSpec(a.shape, lambda i, nd=a.ndim: (0,) * nd) for a in args],
        out_specs=pl.BlockSpec((nch + 1, db, KC), lambda i: (0, 0, 0)),
        out_shape=jax.ShapeDtypeStruct((nch + 1, db, KC), F32),
        scratch_shapes=[pltpu.VMEM((nch + 1, db, KC), I32)],
        compiler_params=_cparams(("arbitrary",)),
        name="a_select",
    )(*args)


def _paged_attn_kernel(pt_ref, *refs, pps, k_feature_major, v_mode, n_interleave, has_mask, has_bias):
    n_kparts = len(k_feature_major)
    g = pl.program_id(1)
    ng = pl.num_programs(1)
    it = iter(refs)
    q_refs = [next(it) for _ in range(n_kparts)]
    kn_refs = [next(it) for _ in range(n_kparts)]
    vn_ref = next(it)
    an_ref = next(it)
    bias_ref = next(it) if has_bias else None
    mask_ref = next(it) if has_mask else None
    k_pages = [[next(it) for _ in range(pps)] for _ in range(n_kparts)]
    v_pages = k_pages[0] if v_mode == 'share0' else [next(it) for _ in range(pps)]
    o_ref, m_ref, l_ref, acc_ref = next(it), next(it), next(it), next(it)
    rows = acc_ref.shape[0]

    @pl.when(g == 0)
    def _():
        s_new = an_ref[0]
        for q_ref, kn_ref in zip(q_refs, kn_refs):
            s_new = s_new + jnp.sum(q_ref[0].astype(F32) * kn_ref[0], axis=-1, keepdims=True)
        m_ref[...] = s_new
        l_ref[...] = jnp.ones_like(l_ref)
        acc_ref[...] = jnp.broadcast_to(vn_ref[0], acc_ref.shape)

    parts = []
    shared = []
    for p in range(pps):
        s = None
        for part in range(n_kparts):
            kb = k_pages[part][p][...].astype(BF16)
            if part == 0 and v_mode == 'share0':
                shared.append(kb)
            if k_feature_major[part]:
                d = jnp.dot(q_refs[part][0], kb, preferred_element_type=F32)
            else:
                d = lax.dot_general(q_refs[part][0], kb, _NT, preferred_element_type=F32)
            s = d if s is None else s + d
        parts.append(s)
    s = jnp.concatenate(parts, axis=-1)
    if has_bias:
        s = s + bias_ref[...]
    if has_mask:
        s = s + mask_ref[0]
    m = m_ref[...]
    m_new = jnp.maximum(m, jnp.max(s, axis=-1, keepdims=True))
    alpha = jnp.exp(m - m_new)
    p_all = jnp.exp(s - m_new)
    l_ref[...] = alpha * l_ref[...] + jnp.sum(p_all, axis=-1, keepdims=True)
    pb = p_all.astype(BF16)
    acc = alpha * acc_ref[...]
    if v_mode == 'interleaved':
        row_kv = lax.broadcasted_iota(I32, acc.shape, 0) // (rows // n_interleave)
    for p in range(pps):
        pp = pb[:, p * PAGE_SIZE:(p + 1) * PAGE_SIZE]
        if v_mode == 'share0':
            acc = acc + jnp.dot(pp, shared[p], preferred_element_type=F32)
        elif v_mode == 'feature_major':
            acc = acc + lax.dot_general(pp, v_pages[p][...].astype(BF16), _NT, preferred_element_type=F32)
        else:
            upd = jnp.zeros(acc.shape, F32)
            for kv in range(n_interleave):
                v_kv = v_pages[p][pl.ds(kv, PAGE_SIZE, stride=n_interleave), :].astype(BF16)
                upd = jnp.where(row_kv == kv, jnp.dot(pp, v_kv, preferred_element_type=F32), upd)
            acc = acc + upd
    acc_ref[...] = acc
    m_ref[...] = m_new

    @pl.when(g == ng - 1)
    def _():
        o_ref[0] = acc_ref[...] / l_ref[...]


def _paged_attn(page_table, slot, q_parts, knew_parts, v_new, add_new, bias, mask, k_pools, k_feature_major,
                v_pool, v_mode, n_interleave=1):
    db, rows = q_parts[0].shape[:2]
    n_pages = page_table.shape[1]
    page_bytes = sum(int(np.prod(p.shape[2:])) * 4 for p in list(k_pools) + ([] if v_pool is None else [v_pool]))
    pps = n_pages
    while pps > 1 and (2 * pps * page_bytes > PAGE_VMEM_BUDGET or pps > MAX_PAGES_PER_STEP):
        pps //= 2
    dv = v_new.shape[-1]
    in_specs, args = [], []
    for q in q_parts:
        in_specs.append(pl.BlockSpec((1, rows, q.shape[-1]), lambda i, g, pt: (i, 0, 0)))
        args.append(q)
    for kn in knew_parts + [v_new]:
        in_specs.append(pl.BlockSpec((1,) + kn.shape[1:], lambda i, g, pt: (i, 0, 0)))
        args.append(kn)
    in_specs.append(pl.BlockSpec((1, rows, 1), lambda i, g, pt: (i, 0, 0)))
    args.append(add_new)
    if bias is not None:
        in_specs.append(pl.BlockSpec((rows, pps * PAGE_SIZE), lambda i, g, pt: (0, g)))
        args.append(bias)
    if mask is not None:
        in_specs.append(pl.BlockSpec((1, 1, pps * PAGE_SIZE), lambda i, g, pt: (i, 0, g)))
        args.append(mask)
    for pool in k_pools:
        in_specs += _page_specs(tuple(pool.shape[2:]), slot, n_pages, pps)
        args += [pool] * pps
    if v_mode != 'share0':
        in_specs += _page_specs(tuple(v_pool.shape[2:]), slot, n_pages, pps)
        args += [v_pool] * pps
    grid_spec = pltpu.PrefetchScalarGridSpec(
        num_scalar_prefetch=1,
        grid=(db, n_pages // pps),
        in_specs=in_specs,
        out_specs=pl.BlockSpec((1, rows, dv), lambda i, g, pt: (i, 0, 0)),
        scratch_shapes=[pltpu.VMEM((rows, 1), F32), pltpu.VMEM((rows, 1), F32), pltpu.VMEM((rows, dv), F32)],
    )
    return pl.pallas_call(
        functools.partial(_paged_attn_kernel, pps=pps, k_feature_major=tuple(k_feature_major), v_mode=v_mode,
                          n_interleave=n_interleave, has_mask=mask is not None, has_bias=bias is not None),
        grid_spec=grid_spec,
        out_shape=jax.ShapeDtypeStruct((db, rows, dv), F32),
        compiler_params=_cparams(("parallel", "arbitrary")),
        name="paged_attn",
    )(page_table.reshape(-1), *args)


def _c_post_kernel(o_ref, wuv_ref, wo_ref, x_ref, y_ref):
    parts = []
    for h in range(C_HEADS):
        parts.append(jnp.dot(o_ref[h].astype(BF16), wuv_ref[h], preferred_element_type=F32))
    o = jnp.concatenate(parts, axis=-1).astype(BF16)
    y_ref[...] = x_ref[...] + jnp.dot(o, wo_ref[...], preferred_element_type=F32)


def _c_post(o_lat, w_uv, w_o, x):
    args = (o_lat, w_uv, w_o, x)
    return pl.pallas_call(
        _c_post_kernel,
        grid=(1,),
        in_specs=[pl.BlockSpec(a.shape, lambda i, nd=a.ndim: (0,) * nd) for a in args],
        out_specs=pl.BlockSpec(x.shape, lambda i: (0, 0)),
        out_shape=jax.ShapeDtypeStruct(x.shape, F32),
        compiler_params=_cparams(("arbitrary",)),
        name="c_post",
    )(*args)


def _q_lat_kernel(qn_ref, wuk_ref, o_ref):
    for h in range(C_HEADS):
        o_ref[h] = lax.dot_general(qn_ref[h], wuk_ref[h], _NT, preferred_element_type=F32).astype(o_ref.dtype)


def _q_lat(qn, w_uk_h):
    args = (qn, w_uk_h)
    h, m = qn.shape[:2]
    return pl.pallas_call(
        _q_lat_kernel,
        grid=(1,),
        in_specs=[pl.BlockSpec(a.shape, lambda i, nd=a.ndim: (0,) * nd) for a in args],
        out_specs=pl.BlockSpec((h, m, KV_LORA), lambda i: (0, 0, 0)),
        out_shape=jax.ShapeDtypeStruct((h, m, KV_LORA), BF16),
        compiler_params=_cparams(("arbitrary",)),
        name="c_q_lat",
    )(*args)


def _blockdiag_q(q, n_blocks):
    db, nb, g, d = q.shape
    eye = jnp.eye(nb, dtype=q.dtype)
    return jnp.einsum('bkgd,kc->bkgcd', q, eye).reshape(db, nb * g, nb * d)


def _diag_blocks(o, n_blocks, g):
    db, rows, cols = o.shape
    e = cols // n_blocks
    o = o.reshape(db, n_blocks, g, n_blocks, e)
    return jnp.stack([o[:, k, :, k] for k in range(n_blocks)], axis=1)


def _feature_major_pages(pool):
    nd = pool.ndim
    t = jnp.transpose(pool, (0, 1) + tuple(range(3, nd)) + (2,))
    return t.reshape(pool.shape[0], pool.shape[1], -1, pool.shape[2])


def _mixer_a(xp, xs, g, w_in, w_o, rel_table, pool_k, pool_v, pool_i, slot, page_table):
    b, s, d = xp.shape
    db = xs.shape[0]
    wq, wk, wv, wqi, wki, wwi = [w.astype(BF16) for w in jnp.split(w_in, np.cumsum(A_SIZES)[:-1].tolist(), axis=-1)]
    wkw = jnp.pad(jnp.concatenate([wki, wwi], axis=1), ((0, 0), (0, LANE - IDX_DIM - IDX_HEADS)))
    wo = w_o.astype(BF16)
    cols = list(range(A_HEADS))
    q_scale, qi_scale = HEAD_DIM ** -0.5, IDX_DIM ** -0.5
    qh, qit, kt, vt, kw, kwt = _a_proj(xp, g, _pad_heads(wq * q_scale, A_HEADS, HEAD_DIM), wk.T, wv.T,
                                       (wqi * qi_scale).T, wkw)
    topk = min(TOPK_MAX, s // 4)
    tiles, far = _bias_tiles(rel_table, cols)
    tiles = tiles.reshape(A_KV_HEADS, A_GROUP, 2, TQ, KC).transpose(0, 2, 1, 3, 4).reshape(A_KV_HEADS, 2, A_GROUP * TQ, KC)
    attn = _a_prompt_attn(far, qit, kwt, kw, qh, kt, vt, tiles, topk)
    yp = _out_proj(attn.reshape(b * s, -1), wo, xp.reshape(b * s, d), 512).reshape(b, s, d)
    rows_p = (kt.reshape(b, A_KV_HEADS, HEAD_DIM, s).transpose(0, 3, 1, 2),
              vt.reshape(b, A_KV_HEADS, HEAD_DIM, s).transpose(0, 3, 1, 2), kwt[:, :IDX_DIM].transpose(0, 2, 1))
    n_pages = page_table.shape[1]
    past = n_pages * PAGE_SIZE
    q, k, v, qi, kw = _norm_proj(xs.reshape(db, d), g, [wq, wk, wv, wqi, wkw], [BF16, F32, F32, BF16, F32],
                                 [q_scale, 1.0, 1.0, qi_scale, 1.0], db)
    ki, wi = kw[:, :IDX_DIM], kw[:, IDX_DIM:IDX_DIM + IDX_HEADS]
    qi3 = qi.reshape(db, IDX_HEADS, IDX_DIM)
    sc = _a_idx_scores(page_table, jnp.pad(qi3, ((0, 0), (0, IDX_HEADS), (0, 0))),
                       jnp.pad(wi.reshape(db, IDX_HEADS, 1), ((0, 0), (0, IDX_HEADS), (0, 0))),
                       _feature_major_pages(pool_i), slot)
    topk_s = min(TOPK_MAX, (past + 1) // 4)
    ma = _a_select(sc.reshape(db, past), qi3.astype(F32), ki.reshape(db, 1, IDX_DIM), wi, topk_s)
    mask = ma[:past // KC].transpose(1, 0, 2).reshape(db, 1, past)
    tab = rel_table.astype(F32)
    bias = _bias_lookup(tab, past - np.arange(past)).T
    add_new = tab[0][None, :, None] + ma[past // KC, :, 0][:, None, None]
    qm = _blockdiag_q(q.reshape(db, A_KV_HEADS, A_GROUP, HEAD_DIM), A_KV_HEADS)
    o = _paged_attn(page_table, slot, [qm], [k.reshape(db, 1, -1)], v.reshape(db, 1, -1), add_new, bias, mask,
                    [_feature_major_pages(pool_k)], [True], _feature_major_pages(pool_v), 'feature_major')
    o = _diag_blocks(o, A_KV_HEADS, A_GROUP).reshape(db, A_HEADS * HEAD_DIM)
    ys = _out_proj(o, wo, xs.reshape(db, d), db).reshape(db, 1, d)
    rows_s = (k.reshape(db, 1, A_KV_HEADS, HEAD_DIM), v.reshape(db, 1, A_KV_HEADS, HEAD_DIM), ki.reshape(db, 1, IDX_DIM))
    return yp, ys, rows_p, rows_s


def _mixer_b(xp, xs, g, w_in, lam_p, subln, w_o, rel_table, lam_init, pool_k, pool_v, slot, page_table):
    b, s, d = xp.shape
    db = xs.shape[0]
    hd = 2 * HEAD_DIM
    wq, wk, wv = [w.astype(BF16) for w in jnp.split(w_in, np.cumsum(B_SIZES)[:-1].tolist(), axis=-1)]
    wo = w_o.astype(BF16)
    ngrp = B_KV_HEADS * 2
    q_scale = HEAD_DIM ** -0.5
    cols = [2 * (kv * B_GROUP + gg) + j for kv in range(B_KV_HEADS) for j in range(2) for gg in range(B_GROUP)]
    wq_kjg = (wq * q_scale).reshape(d, B_KV_HEADS, B_GROUP, 2, HEAD_DIM).transpose(0, 1, 3, 2, 4).reshape(d, -1)
    qh, kt, v = _b_proj(xp, g, _pad_heads(wq_kjg, B_HEADS * 2, HEAD_DIM), wk.T, wv)
    tiles, far = _bias_tiles(rel_table, cols)
    tiles = tiles.reshape(ngrp, B_GROUP, 2, TQ, KC).transpose(0, 2, 1, 3, 4).reshape(ngrp, 2, B_GROUP * TQ, KC)
    o0, o1 = _b_prompt_attn(far, qh, kt, v, tiles)
    yp = _b_post(o0.reshape(b * s, -1), o1.reshape(b * s, -1), lam_p, subln, wo, xp.reshape(b * s, d),
                 lam_init, 512).reshape(b, s, d)
    rows_p = (kt.reshape(b, B_KV_HEADS, 2, HEAD_DIM, s).transpose(0, 4, 1, 2, 3), v.reshape(b, s, B_KV_HEADS, hd))
    n_pages = page_table.shape[1]
    past = n_pages * PAGE_SIZE
    q, k, v = _norm_proj(xs.reshape(db, d), g, [wq, wk, wv], [BF16, F32, F32], [q_scale, 1.0, 1.0], db)
    tab = rel_table.astype(F32)
    colr = np.asarray(cols).reshape(B_KV_HEADS, 2, B_GROUP).transpose(0, 2, 1).reshape(-1)
    bias = _bias_lookup(tab[:, colr], past - np.arange(past)).T
    add_new = jnp.broadcast_to(tab[0][colr][None, :, None], (db, len(cols), 1))
    q5 = q.reshape(db, B_KV_HEADS, B_GROUP, 2, HEAD_DIM)
    eye = jnp.eye(ngrp, dtype=q.dtype).reshape(B_KV_HEADS, 2, B_KV_HEADS, 2)
    qm = jnp.einsum('bkgjd,kjcm->bkgjcmd', q5, eye).reshape(db, ngrp * B_GROUP, ngrp * HEAD_DIM)
    v_new = jnp.repeat(v.reshape(db, B_KV_HEADS, hd), B_GROUP * 2, axis=1)
    pool_v_rows = pool_v.reshape(pool_v.shape[0], pool_v.shape[1], PAGE_SIZE * B_KV_HEADS, hd)
    o = _paged_attn(page_table, slot, [qm], [k.reshape(db, 1, -1)], v_new, add_new, bias, None,
                    [_feature_major_pages(pool_k)], [True], pool_v_rows, 'interleaved', B_KV_HEADS)
    o = o.reshape(db, B_KV_HEADS, B_GROUP, 2, hd)
    o0 = o[:, :, :, 0].reshape(db, B_HEADS * hd)
    o1 = o[:, :, :, 1].reshape(db, B_HEADS * hd)
    ys = _b_post(o0, o1, lam_p, subln, wo, xs.reshape(db, d), lam_init, db).reshape(db, 1, d)
    rows_s = (k.reshape(db, 1, B_KV_HEADS, 2, HEAD_DIM), v.reshape(db, 1, B_KV_HEADS, hd))
    return yp, ys, rows_p, rows_s


def _mixer_c(xp, xs, g, w_down, q_norm, w_uq, kv_norm, w_ukv, w_o, pool_ckv, pool_kpe, slot, page_table):
    b, s, d = xp.shape
    db = xs.shape[0]
    n_pages = page_table.shape[1]
    past = n_pages * PAGE_SIZE
    wd = jnp.pad(w_down.astype(BF16), ((0, 0), (0, Q_LORA + KV_LORA + LANE - w_down.shape[1])))
    wq_aug = _pad_heads(w_uq.astype(BF16), C_HEADS, C_QK)
    wkv = w_ukv.astype(BF16).reshape(KV_LORA, C_HEADS, NOPE_DIM + V_DIM)
    wuk_aug = _pad_heads(wkv[:, :, :NOPE_DIM].reshape(KV_LORA, -1), C_HEADS, NOPE_DIM)
    wuv_aug = _pad_heads(wkv[:, :, NOPE_DIM:].reshape(KV_LORA, -1), C_HEADS, V_DIM)
    wo = w_o.astype(BF16)
    q, k, v, ckv, kpet = _c_proj(xp, g, wd, q_norm, wq_aug, kv_norm, wuk_aug, wuv_aug, _rope_tables(np.arange(s)), TQ)
    i = np.arange(TQ)[:, None]
    j = np.arange(KC)[None, :]
    attn = _c_prompt_attn(q, k, v, jnp.asarray(np.where(i >= j, 0.0, MASKED).astype(np.float32)))
    yp = _out_proj(attn.reshape(b * s, -1), wo, xp.reshape(b * s, d), 512).reshape(b, s, d)
    rows_p = (ckv, kpet.transpose(0, 2, 1))
    tabs = tuple(jnp.broadcast_to(t, (db, LANE)) for t in _rope_tables(past + np.arange(1)))
    q, _, _, ckv, kpet = _c_proj(xs.reshape(1, db, d), g, wd, q_norm, wq_aug, kv_norm, wuk_aug, wuv_aug, tabs, db)
    q3 = q.reshape(db, C_HEADS, LANE)
    ckv, kpe = ckv.reshape(db, KV_LORA), kpet[0].T
    q_lat = _q_lat(q3[:, :, :NOPE_DIM].transpose(1, 0, 2), wkv[:, :, :NOPE_DIM].transpose(1, 0, 2)).transpose(1, 0, 2)
    o_lat = _paged_attn(page_table, slot, [q_lat, q3[:, :, NOPE_DIM:C_QK]],
                        [ckv.reshape(db, 1, KV_LORA), kpe.reshape(db, 1, ROPE_DIM)], ckv.reshape(db, 1, KV_LORA),
                        jnp.zeros((db, C_HEADS, 1), F32), None, None,
                        [pool_ckv, _feature_major_pages(pool_kpe)], [False, True], None, 'share0')
    ys = _c_post(o_lat.transpose(1, 0, 2), wkv[:, :, NOPE_DIM:].transpose(1, 0, 2), wo, xs.reshape(db, d)).reshape(db, 1, d)
    rows_s = (ckv.reshape(db, 1, KV_LORA), kpe.reshape(db, 1, ROPE_DIM))
    return yp, ys, rows_p, rows_s


def kernel(x_prompt, x_sample, cache_a_k, cache_a_v, cache_a_idx, cache_b_k, cache_b_v, cache_c_ckv, cache_c_kpe, state_conv, page_table, rel_table, norm_mix, norm_ffn, norm_final, a_w_in, a_w_o, b_w_in, b_lambda, b_subln, b_w_o, c_w_down, c_q_norm, c_w_uq, c_kv_norm, c_w_ukv, c_w_o, ffn_w_up, ffn_conv_w, ffn_conv_b, ffn_w_down):
    depth = norm_mix.shape[0]
    db = x_sample.shape[0]
    xp, xs = x_prompt, x_sample
    rows_p = {0: [], 1: [], 2: []}
    rows_s = {0: [], 1: [], 2: []}
    conv_p, conv_s = [], []
    for i in range(depth):
        kind, slot = i % N_MIXERS, i // N_MIXERS
        if kind == 0:
            yp, ys, rp, rs = _mixer_a(xp, xs, norm_mix[i], a_w_in[slot], a_w_o[slot], rel_table,
                                      cache_a_k, cache_a_v, cache_a_idx, slot, page_table)
        elif kind == 1:
            lam_init = 0.8 - 0.6 * math.exp(-0.3 * i)
            yp, ys, rp, rs = _mixer_b(xp, xs, norm_mix[i], b_w_in[slot], b_lambda[slot], b_subln[slot], b_w_o[slot],
                                      rel_table, lam_init, cache_b_k, cache_b_v, slot, page_table)
        else:
            yp, ys, rp, rs = _mixer_c(xp, xs, norm_mix[i], c_w_down[slot], c_q_norm[slot], c_w_uq[slot],
                                      c_kv_norm[slot], c_w_ukv[slot], c_w_o[slot],
                                      cache_c_ckv, cache_c_kpe, slot, page_table)
        rows_p[kind].append(rp)
        rows_s[kind].append(rs)
        xp, xs = yp, ys
        fw = _ffn_weights(ffn_w_up[i], ffn_conv_w[i], ffn_conv_b[i], ffn_w_down[i])
        last = i == depth - 1
        xp, cp = _ffn_prompt(xp, norm_ffn[i], fw, norm_final, last, 512)
        xs2, cs = _ffn_sample(xs.reshape(db, -1), norm_ffn[i], fw, norm_final, last, state_conv[i])
        xs = xs2.reshape(xs.shape)
        conv_p.append(cp)
        conv_s.append(cs)

    def stack(rows, j):
        return jnp.stack([r[j] for r in rows])

    a_p, b_p, c_p = rows_p[0], rows_p[1], rows_p[2]
    a_s, b_s, c_s = rows_s[0], rows_s[1], rows_s[2]
    return (xp, xs,
            stack(a_p, 0), stack(a_p, 1), stack(a_p, 2), stack(b_p, 0), stack(b_p, 1),
            stack(c_p, 0), stack(c_p, 1), jnp.stack(conv_p),
            stack(a_s, 0), stack(a_s, 1), stack(a_s, 2), stack(b_s, 0), stack(b_s, 1),
            stack(c_s, 0), stack(c_s, 1), jnp.stack(conv_s))
```
